```python
import jax, jax.numpy as jnp
from jax import lax
import numpy as np

D_MODEL = 2048
BATCH = 32
SEQ = 256
DEPTH = 4
DEC_BATCH = 4
DEC_SEQ = 1024
PAST_LEN = 256

GRID_W = 64
WA = 512
WB = 512
N_POOL_GROUPS = 4
POOL_GC = WB // N_POOL_GROUPS
POOL_WINDOWS = (2, 4, 8, 16)
WC = 1024
N_HEADS = 4
HEAD_DIM = WC // N_HEADS
CONV_A_WIDTH = 31
FFN_CONV_WIDTH = 3
D_FF = 5632
CHUNK = 64
N_BRANCH = 3
N_MOD = 6
EPS = 1e-6
FORGET_BIAS = 3.0

OFF_GLU = 0
OFF_POOL = OFF_GLU + 2 * WA
OFF_QKV = OFF_POOL + WB
OFF_OG = OFF_QKV + 3 * WC
OFF_GATES = OFF_OG + WC
OFF_MERGE = OFF_GATES + 4 * N_HEADS
N_IN = OFF_MERGE + N_BRANCH * D_MODEL

kernel_name = 'hybrid_diffusion_conv_pool_mlstm_step'


def rmsnorm(x, w):
    xf = x.astype(jnp.float32)
    y = xf * lax.rsqrt(jnp.mean(xf * xf, axis=-1, keepdims=True) + EPS)
    return y.astype(x.dtype) * w


def layernorm(x, w, b):
    xf = x.astype(jnp.float32)
    mu = jnp.mean(xf, axis=-1, keepdims=True)
    var = jnp.mean(jnp.square(xf - mu), axis=-1, keepdims=True)
    return ((xf - mu) * lax.rsqrt(var + EPS)).astype(x.dtype) * w + b


def dwconv(x, w, b):
    y = lax.conv_general_dilated(x, w[:, None, :].astype(x.dtype), (1,), 'SAME',
                                 dimension_numbers=('NWC', 'WIO', 'NWC'),
                                 feature_group_count=x.shape[-1])
    return y + b


def along_rows(fn, x, rows):
    if rows is None:
        return fn(x)
    B, L, C = x.shape
    return fn(x.reshape(B * rows, GRID_W, C)).reshape(B, L, -1)


def along_cols(fn, x, rows):
    if rows is None:
        return fn(x)
    B, L, C = x.shape
    xc = x.reshape(B, rows, GRID_W, C).transpose(0, 2, 1, 3).reshape(B * GRID_W, rows, C)
    y = fn(xc)
    return y.reshape(B, GRID_W, rows, -1).transpose(0, 2, 1, 3).reshape(B, L, -1)


def centred_pool_minus_self(x, window):
    L = x.shape[1]
    xf = x.astype(jnp.float32)
    csum = jnp.concatenate([jnp.zeros_like(xf[:, :1]), jnp.cumsum(xf, axis=1)], axis=1)
    t = jnp.arange(L)
    lo = jnp.clip(t - window // 2, 0, L)
    hi = jnp.clip(t - window // 2 + window, 0, L)
    mean = (csum[:, hi] - csum[:, lo]) / (hi - lo).astype(jnp.float32)[None, :, None]
    return (mean - xf).astype(x.dtype)


def pool_groups(x):
    return jnp.concatenate([centred_pool_minus_self(x[..., g * POOL_GC:(g + 1) * POOL_GC], POOL_WINDOWS[g])
                            for g in range(N_POOL_GROUPS)], axis=-1)


def mlstm_scan(q, k, v, log_i, log_f, C0, n0, m0):
    B, L, H, Dh = q.shape
    nc = L // CHUNK

    def chunks(a):
        return a.reshape(B, nc, CHUNK, H, Dh).transpose(1, 0, 3, 2, 4)

    def gchunks(a):
        return a.reshape(B, nc, CHUNK, H).transpose(1, 0, 3, 2)

    causal = jnp.tril(jnp.ones((CHUNK, CHUNK), dtype=bool))

    def step(carry, inp):
        C, n, m = carry
        qc, kc, vc, li, lf = inp
        b = jnp.cumsum(lf, axis=-1)
        logd = jnp.where(causal, b[..., :, None] - b[..., None, :] + li[..., None, :], -jnp.inf)
        inter = b + m[..., None]
        m_t = jnp.maximum(inter, jnp.max(logd, axis=-1))
        w_int = jnp.exp(inter - m_t)
        s = jnp.einsum('bhtd,bhsd->bhts', qc, kc) * jnp.exp(logd - m_t[..., None])
        num = w_int[..., None] * jnp.einsum('bhtd,bhde->bhte', qc, C) + jnp.einsum('bhts,bhse->bhte', s, vc)
        den = w_int * jnp.einsum('bhtd,bhd->bht', qc, n) + jnp.sum(s, axis=-1)
        h = num / jnp.maximum(jnp.abs(den), jnp.exp(-m_t))[..., None]
        b_last = b[..., -1]
        g = b_last[..., None] - b + li
        m_new = jnp.maximum(b_last + m, jnp.max(g, axis=-1))
        decay = jnp.exp(b_last + m - m_new)
        wk = kc * jnp.exp(g - m_new[..., None])[..., None]
        C_new = decay[..., None, None] * C + jnp.einsum('bhsd,bhse->bhde', wk, vc)
        n_new = decay[..., None] * n + jnp.sum(wk, axis=2)
        return (C_new, n_new, m_new), h

    carry0 = (C0.astype(jnp.float32), n0.astype(jnp.float32), m0.astype(jnp.float32))
    (C, n, m), h = lax.scan(step, carry0, (chunks(q), chunks(k), chunks(v), gchunks(log_i), gchunks(log_f)))
    h = h.transpose(1, 0, 3, 2, 4).reshape(B, L, H, Dh)
    return h, (C, n, m)


def mlstm_bidir(q, k, v, gates, C0, n0, m0):
    i_f, f_f, i_b, f_b = jnp.split(gates, 4, axis=-1)
    h_f, (Cf, nf, mf) = mlstm_scan(q, k, v, i_f, jax.nn.log_sigmoid(f_f), C0[:, 0], n0[:, 0], m0[:, 0])
    flip = lambda a: jnp.flip(a, axis=1)
    h_b, (Cb, nb, mb) = mlstm_scan(flip(q), flip(k), flip(v), flip(i_b), flip(jax.nn.log_sigmoid(f_b)),
                                   C0[:, 1], n0[:, 1], m0[:, 1])
    h = h_f + flip(h_b)
    return h, (jnp.stack([Cf, Cb], axis=1), jnp.stack([nf, nb], axis=1), jnp.stack([mf, mb], axis=1))


def trunk_layer(x, cond, C0, n0, m0, rows, w_ada, b_ada, norm_mix_pre, norm_mix_post, norm_ffn_pre,
                norm_ffn_post, w_in, b_in, conv_a_w, conv_a_b, ln_a_w, ln_a_b, w_a_out, w_pool, pool_scale,
                mlstm_norm_w, w_c_out, w_out, w_ffn_up, ffn_conv_w, ffn_conv_b, w_ffn_down):
    B, L, _ = x.shape
    mod = (jax.nn.silu(cond) @ w_ada + b_ada)[:, None, :]
    sh1, sc1, g1, sh2, sc2, g2 = jnp.split(mod, N_MOD, axis=-1)

    h = rmsnorm(x, norm_mix_pre) * (1 + sc1) + sh1
    z = h @ w_in + b_in

    glu = z[..., OFF_GLU:OFF_GLU + WA] * jax.nn.sigmoid(z[..., OFF_GLU + WA:OFF_POOL])
    a = along_rows(lambda t: dwconv(t, conv_a_w, conv_a_b), glu, rows)
    y_a = jax.nn.silu(layernorm(a, ln_a_w, ln_a_b)) @ w_a_out

    p = along_cols(pool_groups, z[..., OFF_POOL:OFF_QKV], rows)
    y_b = jnp.einsum('blgc,gcd->blgd', p.reshape(B, L, N_POOL_GROUPS, POOL_GC), w_pool).reshape(B, L, D_MODEL) * pool_scale

    q, k, v = [z[..., OFF_QKV + j * WC:OFF_QKV + (j + 1) * WC].astype(jnp.float32).reshape(B, L, N_HEADS, HEAD_DIM)
               for j in range(3)]
    q = q * (HEAD_DIM ** -0.5)
    hc, new_state = mlstm_bidir(q, k, v, z[..., OFF_GATES:OFF_MERGE].astype(jnp.float32), C0, n0, m0)
    hc = hc * lax.rsqrt(jnp.mean(hc * hc, axis=-1, keepdims=True) + EPS)
    hc = hc.reshape(B, L, WC).astype(x.dtype) * mlstm_norm_w * jax.nn.sigmoid(z[..., OFF_OG:OFF_GATES])
    y_c = hc @ w_c_out

    ga, gb, gc = jnp.split(jax.nn.sigmoid(z[..., OFF_MERGE:]), N_BRANCH, axis=-1)
    mix = (ga * y_a + gb * y_b + gc * y_c) @ w_out
    x = x + g1 * rmsnorm(mix, norm_mix_post)

    h2 = rmsnorm(x, norm_ffn_pre) * (1 + sc2) + sh2
    u, g = jnp.split(h2 @ w_ffn_up, 2, axis=-1)
    g = along_cols(lambda t: dwconv(t, ffn_conv_w, ffn_conv_b), g, rows)
    f = (jax.nn.gelu(g) * u) @ w_ffn_down
    x = x + g2 * rmsnorm(f, norm_ffn_post)
    return x, new_state


def setup_inputs(seed: int = 0) -> dict:
    key = jax.random.key(seed)
    keys = jax.random.split(key, 32)
    counter = [0]

    def nrm(shape, scale):
        kk = keys[counter[0]]
        counter[0] += 1
        return scale * jax.random.normal(kk, shape, jnp.float32)

    D = D_MODEL
    x_prompt = nrm((BATCH, SEQ, D), 1.0)
    x_sample = nrm((DEC_BATCH, DEC_SEQ, D), 1.0)
    state_C = nrm((DEC_BATCH, DEPTH, 2, N_HEADS, HEAD_DIM, HEAD_DIM), 0.05)
    state_n = nrm((DEC_BATCH, DEPTH, 2, N_HEADS, HEAD_DIM), 0.5)
    state_m = nrm((DEC_BATCH, DEPTH, 2, N_HEADS), 0.5)
    c = nrm((DEC_BATCH, D), 1.0)
    c_ctx = nrm((D,), 1.0)
    w_ada = nrm((DEPTH, D, N_MOD * D), 0.5 * D ** -0.5)
    b_ada = nrm((DEPTH, N_MOD * D), 0.01)
    norm_mix_pre = 1.0 + nrm((DEPTH, D), 0.05)
    norm_mix_post = 1.0 + nrm((DEPTH, D), 0.05)
    norm_ffn_pre = 1.0 + nrm((DEPTH, D), 0.05)
    norm_ffn_post = 1.0 + nrm((DEPTH, D), 0.05)
    w_in = nrm((DEPTH, D, N_IN), D ** -0.5)
    b_in = nrm((DEPTH, N_IN), 0.01)
    forget_cols = jnp.concatenate([jnp.arange(OFF_GATES + N_HEADS, OFF_GATES + 2 * N_HEADS),
                                   jnp.arange(OFF_GATES + 3 * N_HEADS, OFF_GATES + 4 * N_HEADS)])
    b_in = b_in.at[:, forget_cols].add(FORGET_BIAS)
    conv_a_w = nrm((DEPTH, CONV_A_WIDTH, WA), CONV_A_WIDTH ** -0.5)
    conv_a_b = nrm((DEPTH, WA), 0.01)
    ln_a_w = 1.0 + nrm((DEPTH, WA), 0.05)
    ln_a_b = nrm((DEPTH, WA), 0.01)
    w_a_out = nrm((DEPTH, WA, D), WA ** -0.5)
    w_pool = nrm((DEPTH, N_POOL_GROUPS, POOL_GC, D // N_POOL_GROUPS), POOL_GC ** -0.5)
    pool_scale = 1.0 + nrm((DEPTH, D), 0.05)
    mlstm_norm_w = 1.0 + nrm((DEPTH, WC), 0.05)
    w_c_out = nrm((DEPTH, WC, D), WC ** -0.5)
    w_out = nrm((DEPTH, D, D), D ** -0.5)
    w_ffn_up = nrm((DEPTH, D, 2 * D_FF), D ** -0.5)
    ffn_conv_w = nrm((DEPTH, FFN_CONV_WIDTH, D_FF), FFN_CONV_WIDTH ** -0.5)
    ffn_conv_b = nrm((DEPTH, D_FF), 0.01)
    w_ffn_down = nrm((DEPTH, D_FF, D), D_FF ** -0.5)
    return {'x_prompt': x_prompt, 'x_sample': x_sample, 'state_C': state_C, 'state_n': state_n,
            'state_m': state_m, 'c': c, 'c_ctx': c_ctx, 'w_ada': w_ada, 'b_ada': b_ada,
            'norm_mix_pre': norm_mix_pre, 'norm_mix_post': norm_mix_post, 'norm_ffn_pre': norm_ffn_pre,
            'norm_ffn_post': norm_ffn_post, 'w_in': w_in, 'b_in': b_in, 'conv_a_w': conv_a_w,
            'conv_a_b': conv_a_b, 'ln_a_w': ln_a_w, 'ln_a_b': ln_a_b, 'w_a_out': w_a_out, 'w_pool': w_pool,
            'pool_scale': pool_scale, 'mlstm_norm_w': mlstm_norm_w, 'w_c_out': w_c_out, 'w_out': w_out,
            'w_ffn_up': w_ffn_up, 'ffn_conv_w': ffn_conv_w, 'ffn_conv_b': ffn_conv_b, 'w_ffn_down': w_ffn_down}


def reference(x_prompt, x_sample, state_C, state_n, state_m, c, c_ctx, w_ada, b_ada, norm_mix_pre,
              norm_mix_post, norm_ffn_pre, norm_ffn_post, w_in, b_in, conv_a_w, conv_a_b, ln_a_w, ln_a_b,
              w_a_out, w_pool, pool_scale, mlstm_norm_w, w_c_out, w_out, w_ffn_up, ffn_conv_w, ffn_conv_b,
              w_ffn_down):
    rows = x_sample.shape[1] // GRID_W
    bp = x_prompt.shape[0]
    C_zero = jnp.zeros((bp, 2, N_HEADS, HEAD_DIM, HEAD_DIM), jnp.float32)
    n_zero = jnp.zeros((bp, 2, N_HEADS, HEAD_DIM), jnp.float32)
    m_zero = jnp.zeros((bp, 2, N_HEADS), jnp.float32)
    cond_ctx = c_ctx[None, :]
    yp = x_prompt
    ys = x_sample
    Cs, ns, ms = [], [], []
    for l in range(DEPTH):
        lw = (w_ada[l], b_ada[l], norm_mix_pre[l], norm_mix_post[l], norm_ffn_pre[l], norm_ffn_post[l],
              w_in[l], b_in[l], conv_a_w[l], conv_a_b[l], ln_a_w[l], ln_a_b[l], w_a_out[l], w_pool[l],
              pool_scale[l], mlstm_norm_w[l], w_c_out[l], w_out[l], w_ffn_up[l], ffn_conv_w[l],
              ffn_conv_b[l], w_ffn_down[l])
        yp, (C1, n1, m1) = trunk_layer(yp, cond_ctx, C_zero, n_zero, m_zero, None, *lw)
        Cs.append(C1)
        ns.append(n1)
        ms.append(m1)
        ys, _ = trunk_layer(ys, c, state_C[:, l], state_n[:, l], state_m[:, l], rows, *lw)
    new_state_C = jnp.stack(Cs, axis=1)
    new_state_n = jnp.stack(ns, axis=1)
    new_state_m = jnp.stack(ms, axis=1)
    return (yp, ys, new_state_C, new_state_n, new_state_m)
```

```python
import functools

import jax
import jax.numpy as jnp
from jax import lax
from jax.experimental import pallas as pl
from jax.experimental.pallas import tpu as pltpu

GRID_W = 64
WA = 512
WB = 512
N_POOL_GROUPS = 4
POOL_GC = WB // N_POOL_GROUPS
POOL_WINDOWS = (2, 4, 8, 16)
WC = 1024
N_HEADS = 4
HEAD_DIM = WC // N_HEADS
CONV_A_WIDTH = 31
CHUNK = 64
N_BRANCH = 3
N_MOD = 6
EPS = 1e-6

OFF_POOL = 2 * WA
OFF_QKV = OFF_POOL + WB
OFF_OG = OFF_QKV + 3 * WC
OFF_GATES = OFF_OG + WC
OFF_MERGE = OFF_GATES + 4 * N_HEADS

LANES = 128
SUBLANES = 8
VMEM_LIMIT_BYTES = 56 * 1024 * 1024

N_COND_ROWS = 8
BF16 = jnp.bfloat16
F32 = jnp.float32


def _params(*sem):
    return pltpu.CompilerParams(dimension_semantics=sem, vmem_limit_bytes=VMEM_LIMIT_BYTES)


def _rms(x):
    return x * lax.rsqrt(jnp.mean(x * x, axis=-1, keepdims=True) + EPS)


def _ada_kernel(c_ref, w_ref, b_ref, o_ref):
    c = c_ref[...]
    a = c * jax.nn.sigmoid(c)
    o_ref[...] = jnp.dot(a, w_ref[...], preferred_element_type=F32) + b_ref[...]


def _ada(cond, w_ada, b_ada):
    depth, d, n = w_ada.shape
    tn = 1024
    return pl.pallas_call(
        _ada_kernel,
        grid=(depth, n // tn),
        in_specs=[pl.BlockSpec((N_COND_ROWS, d), lambda l, j: (0, 0)),
                  pl.BlockSpec((None, d, tn), lambda l, j: (l, 0, j)),
                  pl.BlockSpec((None, 1, tn), lambda l, j: (l, 0, j))],
        out_specs=pl.BlockSpec((None, N_COND_ROWS, tn), lambda l, j: (l, 0, j)),
        out_shape=jax.ShapeDtypeStruct((depth, N_COND_ROWS, n), F32),
        compiler_params=_params("arbitrary", "arbitrary"),
        name="ada",
    )(cond, w_ada, b_ada.reshape(depth, 1, n))


class _Layout:
    def __init__(self, bp, lp, bs, ls, d):
        self.bp, self.lp, self.bs, self.ls, self.d = bp, lp, bs, ls, d
        self.mp = bp * lp
        self.ms = bs * ls
        self.m = self.mp + self.ms

    def mod_spec(self, layer, k, tm, grid_rank):
        mp, ls, bs = self.mp, self.ls, self.bs

        def index(i, *_):
            start = i * tm
            row = jnp.where(start < mp, bs, (start - mp) // ls)
            return ((layer * N_COND_ROWS + row) * N_MOD + k, 0, 0)

        del grid_rank
        return pl.BlockSpec((None, 1, self.d), index)


def _norm_mod_kernel(x_ref, w_ref, sc_ref, sh_ref, o_ref):
    h = _rms(x_ref[...]) * w_ref[...]
    o_ref[...] = (h * (1.0 + sc_ref[...]) + sh_ref[...]).astype(o_ref.dtype)


def _norm_mod(lay, x, w, mod, layer, k_scale, k_shift):
    tm = 512
    d = lay.d
    return pl.pallas_call(
        _norm_mod_kernel,
        grid=(lay.m // tm,),
        in_specs=[pl.BlockSpec((tm, d), lambda i: (i, 0)),
                  pl.BlockSpec((1, d), lambda i: (0, 0)),
                  lay.mod_spec(layer, k_scale, tm, 1),
                  lay.mod_spec(layer, k_shift, tm, 1)],
        out_specs=pl.BlockSpec((tm, d), lambda i: (i, 0)),
        out_shape=jax.ShapeDtypeStruct((lay.m, d), BF16),
        compiler_params=_params("parallel"),
        name="norm_mod",
    )(x, w.reshape(1, d), mod, mod)


def _matmul_bias_kernel(a_ref, w_ref, b_ref, o_ref):
    o_ref[...] = jnp.dot(a_ref[...], w_ref[...], preferred_element_type=F32) + b_ref[...]


def _matmul_bias(a, w, b, tm, tn, name):
    m, k = a.shape
    n = w.shape[1]
    return pl.pallas_call(
        _matmul_bias_kernel,
        grid=(m // tm, n // tn),
        in_specs=[pl.BlockSpec((tm, k), lambda i, j: (i, 0)),
                  pl.BlockSpec((k, tn), lambda i, j: (0, j)),
                  pl.BlockSpec((1, tn), lambda i, j: (0, j))],
        out_specs=pl.BlockSpec((tm, tn), lambda i, j: (i, j)),
        out_shape=jax.ShapeDtypeStruct((m, n), F32),
        compiler_params=_params("parallel", "arbitrary"),
        name=name,
    )(a, w, b.reshape(1, n))


CONV_PAD = 16
CONV_ROWS = 32


def _conv_a_kernel(z_ref, w_ref, b_ref, lw_ref, lb_ref, o_ref, pad_ref, *, seg):
    t = z_ref.shape[0]
    stride = seg + 2 * CONV_PAD
    half = CONV_A_WIDTH // 2
    zeros = jnp.zeros((CONV_PAD, WA), F32)
    for s in range(t // seg):
        base = s * stride
        z = z_ref[pl.ds(s * seg, seg), :]
        pad_ref[pl.ds(base, CONV_PAD), :] = zeros
        pad_ref[pl.ds(base + CONV_PAD, seg), :] = z[:, :WA] * jax.nn.sigmoid(z[:, WA:])
        pad_ref[pl.ds(base + CONV_PAD + seg, CONV_PAD), :] = zeros
    for s in range(t // seg):
        for c in range(seg // CONV_ROWS):
            row0 = s * stride + CONV_PAD + c * CONV_ROWS
            acc = jnp.broadcast_to(b_ref[...], (CONV_ROWS, WA))
            for tap in range(CONV_A_WIDTH):
                acc = acc + pad_ref[pl.ds(row0 + tap - half, CONV_ROWS), :] * w_ref[pl.ds(tap, 1), :]
            mu = jnp.mean(acc, axis=-1, keepdims=True)
            cen = acc - mu
            var = jnp.mean(cen * cen, axis=-1, keepdims=True)
            y = cen * lax.rsqrt(var + EPS) * lw_ref[...] + lb_ref[...]
            y = y * jax.nn.sigmoid(y)
            o_ref[pl.ds(s * seg + c * CONV_ROWS, CONV_ROWS), :] = y.astype(o_ref.dtype)


def _conv_a(z, row0, rows, seg, conv_w, conv_b, ln_w, ln_b):
    t = 256
    nseg = t // seg
    blk0 = row0 // t
    return pl.pallas_call(
        functools.partial(_conv_a_kernel, seg=seg),
        grid=(rows // t,),
        in_specs=[pl.BlockSpec((t, 2 * WA), lambda i: (blk0 + i, 0)),
                  pl.BlockSpec((CONV_A_WIDTH, WA), lambda i: (0, 0)),
                  pl.BlockSpec((1, WA), lambda i: (0, 0)),
                  pl.BlockSpec((1, WA), lambda i: (0, 0)),
                  pl.BlockSpec((1, WA), lambda i: (0, 0))],
        out_specs=pl.BlockSpec((t, WA), lambda i: (i, 0)),
        out_shape=jax.ShapeDtypeStruct((rows, WA), BF16),
        scratch_shapes=[pltpu.VMEM((nseg * (seg + 2 * CONV_PAD), WA), F32)],
        compiler_params=_params("parallel"),
        name=f"conv_a_seg{seg}",
    )(z, conv_w, conv_b.reshape(1, WA), ln_w.reshape(1, WA), ln_b.reshape(1, WA))


POOL_ROWS = 64
POOL_HALF_MAX = max(POOL_WINDOWS) // 2


def _pool_kernel(z_ref, o_ref, pad_ref, *, dil):
    t = z_ref.shape[0]
    npos = t // dil
    padr = POOL_HALF_MAX * dil
    zeros = jnp.zeros((padr, WB), F32)
    pad_ref[pl.ds(0, padr), :] = zeros
    pad_ref[pl.ds(padr, t), :] = z_ref[...]
    pad_ref[pl.ds(padr + t, padr), :] = zeros
    for c in range(t // POOL_ROWS):
        r0 = c * POOL_ROWS
        pos = (r0 + lax.broadcasted_iota(jnp.int32, (POOL_ROWS, POOL_GC), 0)) >> (dil.bit_length() - 1)
        for g, win in enumerate(POOL_WINDOWS):
            lanes = pl.ds(g * POOL_GC, POOL_GC)
            acc = None
            for o in range(-(win // 2), win - win // 2):
                term = pad_ref[pl.ds(padr + r0 + o * dil, POOL_ROWS), lanes]
                acc = term if acc is None else acc + term
            lo = jnp.maximum(pos - win // 2, 0)
            hi = jnp.minimum(pos - win // 2 + win, npos)
            cnt = (hi - lo).astype(F32)
            x = pad_ref[pl.ds(padr + r0, POOL_ROWS), lanes]
            o_ref[pl.ds(r0, POOL_ROWS), lanes] = (acc / cnt - x).astype(o_ref.dtype)


def _pool(z, row0, rows, t, dil):
    blk0 = row0 // t
    return pl.pallas_call(
        functools.partial(_pool_kernel, dil=dil),
        grid=(rows // t,),
        in_specs=[pl.BlockSpec((t, WB), lambda i: (blk0 + i, OFF_POOL // WB))],
        out_specs=pl.BlockSpec((t, WB), lambda i: (i, 0)),
        out_shape=jax.ShapeDtypeStruct((rows, WB), BF16),
        scratch_shapes=[pltpu.VMEM((t + 2 * POOL_HALF_MAX * dil, WB), F32)],
        compiler_params=_params("parallel"),
        name=f"pool_dil{dil}",
    )(z)


def _log_sigmoid(x):
    return jnp.minimum(x, 0.0) - jnp.log1p(jnp.exp(-jnp.abs(x)))


def _mlstm_chunk(direction, hd, cidx, q_ref, k_ref, v_ref, gcol_ref, grow_ref, c_scr, n_scr, m_scr):
    rows = pl.ds(pl.multiple_of(cidx * CHUNK, CHUNK), CHUNK)
    cols = pl.ds(hd * HEAD_DIM, HEAD_DIM)
    q = q_ref[rows, cols] * (HEAD_DIM ** -0.5)
    k = k_ref[rows, cols]
    v = v_ref[rows, cols]
    gi, gf = 2 * direction, 2 * direction + 1
    li_col = gcol_ref[hd, cidx, :, gi:gi + 1]
    lf_col = _log_sigmoid(gcol_ref[hd, cidx, :, gf:gf + 1])
    li_row = grow_ref[hd, cidx, gi:gi + 1, :]
    lf_row = _log_sigmoid(grow_ref[hd, cidx, gf:gf + 1, :])
    t_idx = lax.broadcasted_iota(jnp.int32, (CHUNK, CHUNK), 0)
    s_idx = lax.broadcasted_iota(jnp.int32, (CHUNK, CHUNK), 1)
    seen = (s_idx <= t_idx) if direction == 0 else (s_idx >= t_idx)
    seen_t = (t_idx <= s_idx) if direction == 0 else (t_idx >= s_idx)
    b_col = jnp.sum(jnp.where(seen, lf_row, 0.0), axis=1, keepdims=True)
    b_row = jnp.sum(jnp.where(seen_t, lf_col, 0.0), axis=0, keepdims=True)
    b_tot = jnp.sum(lf_row, axis=1, keepdims=True)
    m_prev = m_scr[direction, hd]
    c_prev = c_scr[direction, hd]
    n_prev = n_scr[direction, hd]

    logd = jnp.where(seen, b_col - b_row + li_row, -jnp.inf)
    inter = b_col + m_prev
    m_t = jnp.maximum(inter, jnp.max(logd, axis=1, keepdims=True))
    w_int = jnp.exp(inter - m_t)
    qk = lax.dot_general(q, k, (((1,), (1,)), ((), ())), preferred_element_type=F32)
    s = qk * jnp.exp(logd - m_t)
    num = w_int * jnp.dot(q, c_prev, preferred_element_type=F32) + jnp.dot(s, v, preferred_element_type=F32)
    den = w_int * jnp.sum(q * n_prev, axis=1, keepdims=True) + jnp.sum(s, axis=1, keepdims=True)
    h = num / jnp.maximum(jnp.abs(den), jnp.exp(-m_t))

    g_row = b_tot - b_row + li_row
    g_col = b_tot - b_col + li_col
    m_new = jnp.maximum(b_tot + m_prev, jnp.max(g_row, axis=1, keepdims=True))
    decay = jnp.exp(b_tot + m_prev - m_new)
    wk = k * jnp.exp(g_col - m_new)
    c_scr[direction, hd] = decay * c_prev + lax.dot_general(
        wk, v, (((0,), (0,)), ((), ())), preferred_element_type=F32)
    n_scr[direction, hd] = decay * n_prev + jnp.sum(wk, axis=0, keepdims=True)
    m_scr[direction, hd] = m_new
    return h


def _mlstm_kernel(*refs, nheads, zero_init, emit_state):
    q_ref, k_ref, v_ref, og_ref, gcol_ref, grow_ref, nw_ref = refs[:7]
    pos = 7
    if not zero_init:
        c0_ref, n0_ref, m0_ref = refs[pos:pos + 3]
        pos += 3
    o_ref = refs[pos]
    pos += 1
    if emit_state:
        co_ref, no_ref, mo_ref = refs[pos:pos + 3]
        pos += 3
    c_scr, n_scr, m_scr, h_scr = refs[pos:pos + 4]

    seq = q_ref.shape[0]
    nc = seq // CHUNK
    if zero_init:
        c_scr[...] = jnp.zeros(c_scr.shape, F32)
        n_scr[...] = jnp.zeros(n_scr.shape, F32)
        m_scr[...] = jnp.zeros(m_scr.shape, F32)
    else:
        c_scr[...] = c0_ref[...]
        n_scr[...] = n0_ref[...]
        m_scr[...] = m0_ref[...]

    def step(i, accumulate):
        for direction, cidx in ((0, i), (1, nc - 1 - i)):
            rows = pl.ds(pl.multiple_of(cidx * CHUNK, CHUNK), CHUNK)
            for hd in range(nheads):
                h = _mlstm_chunk(direction, hd, cidx, q_ref, k_ref, v_ref, gcol_ref, grow_ref,
                                 c_scr, n_scr, m_scr)
                cols = pl.ds(hd * HEAD_DIM, HEAD_DIM)
                if accumulate:
                    h_scr[rows, cols] = h_scr[rows, cols] + h
                else:
                    h_scr[rows, cols] = h

    def first_half(i, carry):
        step(i, False)
        return carry

    def second_half(i, carry):
        step(i, True)
        return carry

    lax.fori_loop(0, nc // 2, first_half, 0)
    lax.fori_loop(nc // 2, nc, second_half, 0)

    for hd in range(nheads):
        cols = pl.ds(hd * HEAD_DIM, HEAD_DIM)
        hn = _rms(h_scr[:, cols])
        o_ref[:, cols] = (hn * nw_ref[:, cols] * jax.nn.sigmoid(og_ref[:, cols])).astype(o_ref.dtype)
    if emit_state:
        co_ref[...] = c_scr[...]
        no_ref[...] = n_scr[...]
        mo_ref[...] = m_scr[...]


def _mlstm(z, gates, norm_w, row0, nb, seq, init_state, layer, emit_state):
    nheads = 2
    hg = N_HEADS // nheads
    nc = seq // CHUNK
    wblk = nheads * HEAD_DIM
    blk0 = row0 // seq
    g = gates.reshape(nb, nc, CHUNK, 4, N_HEADS).transpose(0, 4, 1, 2, 3)
    gcol = g
    grow = g.transpose(0, 1, 2, 4, 3)

    def zcol(off):
        return lambda b, h: (blk0 + b, off // wblk + h)

    in_specs = [pl.BlockSpec((seq, wblk), zcol(OFF_QKV)),
                pl.BlockSpec((seq, wblk), zcol(OFF_QKV + WC)),
                pl.BlockSpec((seq, wblk), zcol(OFF_QKV + 2 * WC)),
                pl.BlockSpec((seq, wblk), zcol(OFF_OG)),
                pl.BlockSpec((None, nheads, nc, CHUNK, 4), lambda b, h: (b, h, 0, 0, 0)),
                pl.BlockSpec((None, nheads, nc, 4, CHUNK), lambda b, h: (b, h, 0, 0, 0)),
                pl.BlockSpec((1, wblk), lambda b, h: (0, h))]
    args = [z, z, z, z, gcol, grow, norm_w.reshape(1, WC)]
    if init_state is not None:
        sc, sn, sm = init_state
        depth = sc.shape[1]
        in_specs += [
            pl.BlockSpec((None, None, 2, nheads, HEAD_DIM, HEAD_DIM), lambda b, h: (b, layer, 0, h, 0, 0)),
            pl.BlockSpec((None, None, 2, nheads, 1, HEAD_DIM), lambda b, h: (b, layer, 0, h, 0, 0)),
            pl.BlockSpec((None, None, 2, nheads, 1, 1), lambda b, h: (b, layer, 0, h, 0, 0))]
        args += [sc, sn.reshape(nb, depth, 2, N_HEADS, 1, HEAD_DIM), sm.reshape(nb, depth, 2, N_HEADS, 1, 1)]
    out_specs = [pl.BlockSpec((seq, wblk), lambda b, h: (b, h))]
    out_shape = [jax.ShapeDtypeStruct((nb * seq, WC), BF16)]
    if emit_state:
        out_specs += [pl.BlockSpec((None, 2, nheads, HEAD_DIM, HEAD_DIM), lambda b, h: (b, 0, h, 0, 0)),
                      pl.BlockSpec((None, 2, nheads, 1, HEAD_DIM), lambda b, h: (b, 0, h, 0, 0)),
                      pl.BlockSpec((None, 2, nheads, 1, 1), lambda b, h: (b, 0, h, 0, 0))]
        out_shape += [jax.ShapeDtypeStruct((nb, 2, N_HEADS, HEAD_DIM, HEAD_DIM), F32),
                      jax.ShapeDtypeStruct((nb, 2, N_HEADS, 1, HEAD_DIM), F32),
                      jax.ShapeDtypeStruct((nb, 2, N_HEADS, 1, 1), F32)]
    outs = pl.pallas_call(
        functools.partial(_mlstm_kernel, nheads=nheads, zero_init=init_state is None, emit_state=emit_state),
        grid=(nb, hg),
        in_specs=in_specs,
        out_specs=out_specs,
        out_shape=out_shape,
        scratch_shapes=[pltpu.VMEM((2, nheads, HEAD_DIM, HEAD_DIM), F32),
                        pltpu.VMEM((2, nheads, 1, HEAD_DIM), F32),
                        pltpu.VMEM((2, nheads, 1, 1), F32),
                        pltpu.VMEM((seq, wblk), F32)],
        compiler_params=_params("parallel", "parallel"),
        name=f"mlstm_seq{seq}",
    )(*args)
    return outs


def _merge_kernel(h_ref, a_ref, p_ref, c_ref, wga_ref, wgb_ref, wgc_ref, bga_ref, bgb_ref, bgc_ref,
                  wa_ref, wp_ref, ps_ref, wc_ref, o_ref):
    h = h_ref[...]

    def gate(w_ref, b_ref):
        return jax.nn.sigmoid(jnp.dot(h, w_ref[...], preferred_element_type=F32) + b_ref[...])

    ya = jnp.dot(a_ref[...], wa_ref[...], preferred_element_type=F32)
    yb = jnp.dot(p_ref[...], wp_ref[...], preferred_element_type=F32) * ps_ref[...]
    yc = jnp.dot(c_ref[...], wc_ref[...], preferred_element_type=F32)
    mix = gate(wga_ref, bga_ref) * ya + gate(wgb_ref, bgb_ref) * yb + gate(wgc_ref, bgc_ref) * yc
    o_ref[...] = mix.astype(o_ref.dtype)


def _merge(h, a_act, p, hc, w_merge, b_merge, w_a_out, w_pool, pool_scale, w_c_out):
    m, d = h.shape
    tm = 512
    tn = d // N_POOL_GROUPS
    nj = d // tn

    def wg(k):
        return pl.BlockSpec((d, tn), lambda i, j: (0, k * nj + j))

    def bg(k):
        return pl.BlockSpec((1, tn), lambda i, j: (0, k * nj + j))

    return pl.pallas_call(
        _merge_kernel,
        grid=(m // tm, nj),
        in_specs=[pl.BlockSpec((tm, d), lambda i, j: (i, 0)),
                  pl.BlockSpec((tm, WA), lambda i, j: (i, 0)),
                  pl.BlockSpec((tm, POOL_GC), lambda i, j: (i, j)),
                  pl.BlockSpec((tm, WC), lambda i, j: (i, 0)),
                  wg(0), wg(1), wg(2), bg(0), bg(1), bg(2),
                  pl.BlockSpec((WA, tn), lambda i, j: (0, j)),
                  pl.BlockSpec((None, POOL_GC, tn), lambda i, j: (j, 0, 0)),
                  pl.BlockSpec((1, tn), lambda i, j: (0, j)),
                  pl.BlockSpec((WC, tn), lambda i, j: (0, j))],
        out_specs=pl.BlockSpec((tm, tn), lambda i, j: (i, j)),
        out_shape=jax.ShapeDtypeStruct((m, d), BF16),
        compiler_params=_params("parallel", "arbitrary"),
        name="merge",
    )(h, a_act, p, hc, w_merge, w_merge, w_merge, b_merge, b_merge, b_merge,
      w_a_out, w_pool, pool_scale.reshape(1, d), w_c_out)


def _residual_and_next(x, y, wpost_ref, gate_ref, wpre_ref, sc_ref, sh_ref, xo_ref, ho_ref):
    xn = x + gate_ref[...] * (_rms(y) * wpost_ref[...])
    xo_ref[...] = xn
    hn = _rms(xn) * wpre_ref[...]
    ho_ref[...] = (hn * (1.0 + sc_ref[...]) + sh_ref[...]).astype(ho_ref.dtype)


def _out_proj_kernel(a_ref, w_ref, x_ref, wpost_ref, gate_ref, wpre_ref, sc_ref, sh_ref, xo_ref, ho_ref):
    y = jnp.dot(a_ref[...], w_ref[...], preferred_element_type=F32)
    _residual_and_next(x_ref[...], y, wpost_ref, gate_ref, wpre_ref, sc_ref, sh_ref, xo_ref, ho_ref)


def _out_proj(lay, mix_in, w_out, x, w_post, w_pre_next, mod, layer):
    d = lay.d
    tm = 256
    row = pl.BlockSpec((1, d), lambda i: (0, 0))
    return pl.pallas_call(
        _out_proj_kernel,
        grid=(lay.m // tm,),
        in_specs=[pl.BlockSpec((tm, d), lambda i: (i, 0)),
                  pl.BlockSpec((d, d), lambda i: (0, 0)),
                  pl.BlockSpec((tm, d), lambda i: (i, 0)),
                  row, lay.mod_spec(layer, 2, tm, 1),
                  row, lay.mod_spec(layer, 4, tm, 1), lay.mod_spec(layer, 3, tm, 1)],
        out_specs=[pl.BlockSpec((tm, d), lambda i: (i, 0)), pl.BlockSpec((tm, d), lambda i: (i, 0))],
        out_shape=[jax.ShapeDtypeStruct((lay.m, d), F32), jax.ShapeDtypeStruct((lay.m, d), BF16)],
        compiler_params=_params("parallel"),
        name="out_proj",
    )(mix_in, w_out, x, w_post.reshape(1, d), mod, w_pre_next.reshape(1, d), mod, mod)


def _gelu_tanh(x):
    return 0.5 * x * (1.0 + jnp.tanh(0.7978845608028654 * (x + 0.044715 * (x * x * x))))


def _ffn_up_kernel(h_ref, wu_ref, wg_ref, cw_ref, cb_ref, o_ref, *, n_seq_tiles, lp, shift_s):
    h = h_ref[...]
    u = jnp.dot(h, wu_ref[...], preferred_element_type=F32)
    g = jnp.dot(h, wg_ref[...], preferred_element_type=F32)
    tm, tn = g.shape
    w0, w1, w2 = cw_ref[0:1, :], cw_ref[1:2, :], cw_ref[2:3, :]

    def finish(prev, nxt):
        gc = w0 * prev + w1 * g + w2 * nxt + cb_ref[...]
        o_ref[...] = (_gelu_tanh(gc) * u).astype(o_ref.dtype)

    @pl.when(pl.program_id(0) < n_seq_tiles)
    def _():
        r = lax.broadcasted_iota(jnp.int32, (tm, tn), 0) & (lp - 1)
        prev = jnp.where(r == 0, 0.0, pltpu.roll(g, 1, 0))
        nxt = jnp.where(r == lp - 1, 0.0, pltpu.roll(g, tm - 1, 0))
        finish(prev, nxt)

    @pl.when(pl.program_id(0) >= n_seq_tiles)
    def _():
        zeros = jnp.zeros((shift_s, tn), F32)
        prev = jnp.concatenate([zeros, g[:tm - shift_s]], axis=0)
        nxt = jnp.concatenate([g[shift_s:], zeros], axis=0)
        finish(prev, nxt)


def _ffn_up(lay, h2, w_up, conv_w, conv_b):
    d = lay.d
    dff = w_up.shape[1] // 2
    tm = lay.ls
    tn = 512
    nj = dff // tn
    return pl.pallas_call(
        functools.partial(_ffn_up_kernel, n_seq_tiles=lay.mp // tm, lp=lay.lp, shift_s=GRID_W),
        grid=(lay.m // tm, nj),
        in_specs=[pl.BlockSpec((tm, d), lambda i, j: (i, 0)),
                  pl.BlockSpec((d, tn), lambda i, j: (0, j)),
                  pl.BlockSpec((d, tn), lambda i, j: (0, nj + j)),
                  pl.BlockSpec((3, tn), lambda i, j: (0, j)),
                  pl.BlockSpec((1, tn), lambda i, j: (0, j))],
        out_specs=pl.BlockSpec((tm, tn), lambda i, j: (i, j)),
        out_shape=jax.ShapeDtypeStruct((lay.m, dff), BF16),
        compiler_params=_params("parallel", "arbitrary"),
        name="ffn_up",
    )(h2, w_up, w_up, conv_w, conv_b.reshape(1, dff))


def _ffn_down_kernel(a_ref, w_ref, x_ref, wpost_ref, gate_ref, wpre_ref, sc_ref, sh_ref, xo_ref, ho_ref,
                     acc_ref):
    k = pl.program_id(1)

    @pl.when(k == 0)
    def _():
        acc_ref[...] = jnp.zeros(acc_ref.shape, F32)

    acc_ref[...] += jnp.dot(a_ref[...], w_ref[...], preferred_element_type=F32)

    @pl.when(k == pl.num_programs(1) - 1)
    def _():
        _residual_and_next(x_ref[...], acc_ref[...], wpost_ref, gate_ref, wpre_ref, sc_ref, sh_ref,
                           xo_ref, ho_ref)


def _ffn_down(lay, f_in, w_down, x, w_post, w_pre_next, mod, layer, next_layer):
    d = lay.d
    dff = w_down.shape[0]
    tm = 512
    tk = 512
    row = pl.BlockSpec((1, d), lambda i, k: (0, 0))
    return pl.pallas_call(
        _ffn_down_kernel,
        grid=(lay.m // tm, dff // tk),
        in_specs=[pl.BlockSpec((tm, tk), lambda i, k: (i, k)),
                  pl.BlockSpec((tk, d), lambda i, k: (k, 0)),
                  pl.BlockSpec((tm, d), lambda i, k: (i, 0)),
                  row, lay.mod_spec(layer, 5, tm, 2),
                  row, lay.mod_spec(next_layer, 1, tm, 2), lay.mod_spec(next_layer, 0, tm, 2)],
        out_specs=[pl.BlockSpec((tm, d), lambda i, k: (i, 0)), pl.BlockSpec((tm, d), lambda i, k: (i, 0))],
        out_shape=[jax.ShapeDtypeStruct((lay.m, d), F32), jax.ShapeDtypeStruct((lay.m, d), BF16)],
        scratch_shapes=[pltpu.VMEM((tm, d), F32)],
        compiler_params=_params("parallel", "arbitrary"),
        name="ffn_down",
    )(f_in, w_down, x, w_post.reshape(1, d), mod, w_pre_next.reshape(1, d), mod, mod)


def kernel(x_prompt, x_sample, state_C, state_n, state_m, c, c_ctx, w_ada, b_ada, norm_mix_pre, norm_mix_post, norm_ffn_pre, norm_ffn_post, w_in, b_in, conv_a_w, conv_a_b, ln_a_w, ln_a_b, w_a_out, w_pool, pool_scale, mlstm_norm_w, w_c_out, w_out, w_ffn_up, ffn_conv_w, ffn_conv_b, w_ffn_down):
    bp, lp, d = x_prompt.shape
    bs, ls, _ = x_sample.shape
    depth = w_ada.shape[0]
    lay = _Layout(bp, lp, bs, ls, d)
    assert bs < N_COND_ROWS and lay.mp % ls == 0 and ls % lp == 0
    assert lp & (lp - 1) == 0 and GRID_W & (GRID_W - 1) == 0

    cond = jnp.zeros((N_COND_ROWS, d), F32).at[:bs].set(c).at[bs].set(c_ctx)
    mod = _ada(cond, w_ada, b_ada).reshape(depth * N_COND_ROWS * N_MOD, 1, d)

    w_main = w_in[:, :, :OFF_GATES].astype(BF16)
    w_gate = jnp.pad(w_in[:, :, OFF_GATES:OFF_MERGE], ((0, 0), (0, 0), (0, LANES - 4 * N_HEADS))).astype(BF16)
    b_gate = jnp.pad(b_in[:, OFF_GATES:OFF_MERGE], ((0, 0), (0, LANES - 4 * N_HEADS)))
    w_merge = w_in[:, :, OFF_MERGE:].astype(BF16)
    w_a_out_b = w_a_out.astype(BF16)
    w_pool_b = w_pool.astype(BF16)
    w_c_out_b = w_c_out.astype(BF16)
    w_out_b = w_out.astype(BF16)
    w_up_b = w_ffn_up.astype(BF16)
    w_down_b = w_ffn_down.astype(BF16)

    x = jnp.concatenate([x_prompt.reshape(lay.mp, d), x_sample.reshape(lay.ms, d)], axis=0)
    h = _norm_mod(lay, x, norm_mix_pre[0], mod, 0, 1, 0)
    states = []
    for l in range(depth):
        z = _matmul_bias(h, w_main[l], b_in[l, :OFF_GATES], 1024, 512, "in_proj")
        zg = _matmul_bias(h, w_gate[l], b_gate[l], 1024, LANES, "gate_proj")[:, :4 * N_HEADS]

        a_p = _conv_a(z, 0, lay.mp, lp, conv_a_w[l], conv_a_b[l], ln_a_w[l], ln_a_b[l])
        a_s = _conv_a(z, lay.mp, lay.ms, GRID_W, conv_a_w[l], conv_a_b[l], ln_a_w[l], ln_a_b[l])
        p_p = _pool(z, 0, lay.mp, lp, 1)
        p_s = _pool(z, lay.mp, lay.ms, ls, GRID_W)
        hc_p, c_new, n_new, m_new = _mlstm(z, zg[:lay.mp], mlstm_norm_w[l], 0, bp, lp, None, l, True)
        (hc_s,) = _mlstm(z, zg[lay.mp:], mlstm_norm_w[l], lay.mp, bs, ls, (state_C, state_n, state_m), l, False)
        states.append((c_new, n_new.reshape(bp, 2, N_HEADS, HEAD_DIM), m_new.reshape(bp, 2, N_HEADS)))

        mix_in = _merge(h, jnp.concatenate([a_p, a_s], axis=0), jnp.concatenate([p_p, p_s], axis=0),
                        jnp.concatenate([hc_p, hc_s], axis=0), w_merge[l], b_in[l, OFF_MERGE:].reshape(1, -1),
                        w_a_out_b[l], w_pool_b[l], pool_scale[l], w_c_out_b[l])
        x, h2 = _out_proj(lay, mix_in, w_out_b[l], x, norm_mix_post[l], norm_ffn_pre[l], mod, l)
        f_in = _ffn_up(lay, h2, w_up_b[l], ffn_conv_w[l], ffn_conv_b[l])
        nl = min(l + 1, depth - 1)
        x, h = _ffn_down(lay, f_in, w_down_b[l], x, norm_ffn_post[l], norm_mix_pre[nl], mod, l, nl)

    y_prompt = x[:lay.mp].reshape(bp, lp, d)
    y_sample = x[lay.mp:].reshape(bs, ls, d)
    new_c = jnp.stack([s[0] for s in states], axis=1)
    new_n = jnp.stack([s[1] for s in states], axis=1)
    new_m = jnp.stack([s[2] for s in states], axis=1)
    return (y_prompt, y_sample, new_c, new_n, new_m)
```

```python
import functools

import jax
import jax.numpy as jnp
from jax import lax
from jax.experimental import pallas as pl
from jax.experimental.pallas import tpu as pltpu

GRID_W = 64
WA = 512
WB = 512
N_POOL_GROUPS = 4
POOL_GC = WB // N_POOL_GROUPS
POOL_WINDOWS = (2, 4, 8, 16)
WC = 1024
N_HEADS = 4
HEAD_DIM = WC // N_HEADS
CONV_A_WIDTH = 31
CHUNK = 64
N_BRANCH = 3
N_MOD = 6
EPS = 1e-6

OFF_POOL = 2 * WA
OFF_QKV = OFF_POOL + WB
OFF_OG = OFF_QKV + 3 * WC
OFF_GATES = OFF_OG + WC
OFF_MERGE = OFF_GATES + 4 * N_HEADS

Z_QKV = 0
Z_OG = Z_QKV + 3 * WC
Z_GLU = Z_OG + WC
Z_POOL = Z_GLU + 2 * WA
Z_WIDTH = Z_POOL + WB

LANES = 128
SUBLANES = 8
VMEM_LIMIT_BYTES = 56 * 1024 * 1024

N_COND_ROWS = 8
BF16 = jnp.bfloat16
F32 = jnp.float32


def _params(*sem):
    return pltpu.CompilerParams(dimension_semantics=sem, vmem_limit_bytes=VMEM_LIMIT_BYTES)


def _rms(x):
    return x * lax.rsqrt(jnp.mean(x * x, axis=-1, keepdims=True) + EPS)


def _ada_kernel(c_ref, w_ref, b_ref, o_ref):
    c = c_ref[...]
    a = c * jax.nn.sigmoid(c)
    o_ref[...] = jnp.dot(a, w_ref[...], preferred_element_type=F32) + b_ref[...]


def _ada(cond, w_ada, b_ada):
    depth, d, n = w_ada.shape
    tn = 1024
    return pl.pallas_call(
        _ada_kernel,
        grid=(depth, n // tn),
        in_specs=[pl.BlockSpec((N_COND_ROWS, d), lambda l, j: (0, 0)),
                  pl.BlockSpec((None, d, tn), lambda l, j: (l, 0, j)),
                  pl.BlockSpec((None, 1, tn), lambda l, j: (l, 0, j))],
        out_specs=pl.BlockSpec((None, N_COND_ROWS, tn), lambda l, j: (l, 0, j)),
        out_shape=jax.ShapeDtypeStruct((depth, N_COND_ROWS, n), F32),
        compiler_params=_params("arbitrary", "arbitrary"),
        name="ada",
    )(cond, w_ada, b_ada.reshape(depth, 1, n))


class _Layout:
    def __init__(self, bp, lp, bs, ls, d):
        self.bp, self.lp, self.bs, self.ls, self.d = bp, lp, bs, ls, d
        self.mp = bp * lp
        self.ms = bs * ls
        self.m = self.mp + self.ms

    def mod_spec(self, layer, k, tm):
        mp, ls, bs = self.mp, self.ls, self.bs

        def index(i, *_):
            start = i * tm
            row = jnp.where(start < mp, bs, (start - mp) // ls)
            return ((layer * N_COND_ROWS + row) * N_MOD + k, 0, 0)

        return pl.BlockSpec((None, 1, self.d), index)


def _norm_mod_kernel(x_ref, w_ref, sc_ref, sh_ref, o_ref):
    h = _rms(x_ref[...]) * w_ref[...]
    o_ref[...] = (h * (1.0 + sc_ref[...]) + sh_ref[...]).astype(o_ref.dtype)


def _norm_mod(lay, x, w, mod, layer, k_scale, k_shift):
    tm = 512
    d = lay.d
    return pl.pallas_call(
        _norm_mod_kernel,
        grid=(lay.m // tm,),
        in_specs=[pl.BlockSpec((tm, d), lambda i: (i, 0)),
                  pl.BlockSpec((1, d), lambda i: (0, 0)),
                  lay.mod_spec(layer, k_scale, tm),
                  lay.mod_spec(layer, k_shift, tm)],
        out_specs=pl.BlockSpec((tm, d), lambda i: (i, 0)),
        out_shape=jax.ShapeDtypeStruct((lay.m, d), BF16),
        compiler_params=_params("parallel"),
        name="norm_mod",
    )(x, w.reshape(1, d), mod, mod)


def _matmul_bias_kernel(a_ref, w_ref, b_ref, o_ref):
    o_ref[...] = jnp.dot(a_ref[...], w_ref[...], preferred_element_type=F32) + b_ref[...]


def _matmul_bias(a, w, b, tm, tn, name):
    m, k = a.shape
    n = w.shape[1]
    return pl.pallas_call(
        _matmul_bias_kernel,
        grid=(m // tm, n // tn),
        in_specs=[pl.BlockSpec((tm, k), lambda i, j: (i, 0)),
                  pl.BlockSpec((k, tn), lambda i, j: (0, j)),
                  pl.BlockSpec((1, tn), lambda i, j: (0, j))],
        out_specs=pl.BlockSpec((tm, tn), lambda i, j: (i, j)),
        out_shape=jax.ShapeDtypeStruct((m, n), F32),
        compiler_params=_params("parallel", "arbitrary"),
        name=name,
    )(a, w, b.reshape(1, n))


def _into(prev):
    if prev is None:
        return [], [], {}
    return [prev], [pl.BlockSpec(memory_space=pl.ANY)], {0: 0}


def _skip_refs(kernel_fn, n):
    if n == 0:
        return kernel_fn
    return lambda *refs: kernel_fn(*refs[n:])


CONV_PAD = 16
CONV_ROWS = 32


def _conv_a_kernel(z_ref, w_ref, b_ref, lw_ref, lb_ref, o_ref, pad_ref, *, seg):
    t = z_ref.shape[0]
    stride = seg + 2 * CONV_PAD
    half = CONV_A_WIDTH // 2
    zeros = jnp.zeros((CONV_PAD, WA), F32)
    for s in range(t // seg):
        base = s * stride
        z = z_ref[pl.ds(s * seg, seg), :]
        pad_ref[pl.ds(base, CONV_PAD), :] = zeros
        pad_ref[pl.ds(base + CONV_PAD, seg), :] = z[:, :WA] * jax.nn.sigmoid(z[:, WA:])
        pad_ref[pl.ds(base + CONV_PAD + seg, CONV_PAD), :] = zeros
    for s in range(t // seg):
        for c in range(seg // CONV_ROWS):
            row0 = s * stride + CONV_PAD + c * CONV_ROWS
            acc = jnp.broadcast_to(b_ref[...], (CONV_ROWS, WA))
            for tap in range(CONV_A_WIDTH):
                acc = acc + pad_ref[pl.ds(row0 + tap - half, CONV_ROWS), :] * w_ref[pl.ds(tap, 1), :]
            mu = jnp.mean(acc, axis=-1, keepdims=True)
            cen = acc - mu
            var = jnp.mean(cen * cen, axis=-1, keepdims=True)
            y = cen * lax.rsqrt(var + EPS) * lw_ref[...] + lb_ref[...]
            y = y * jax.nn.sigmoid(y)
            o_ref[pl.ds(s * seg + c * CONV_ROWS, CONV_ROWS), :] = y.astype(o_ref.dtype)


def _conv_a(z, row0, rows, seg, conv_w, conv_b, ln_w, ln_b, prev):
    t = 256
    nseg = t // seg
    blk0 = row0 // t
    pre_args, pre_specs, aliases = _into(prev)
    return pl.pallas_call(
        _skip_refs(functools.partial(_conv_a_kernel, seg=seg), len(pre_args)),
        grid=(rows // t,),
        in_specs=pre_specs + [pl.BlockSpec((t, 2 * WA), lambda i: (blk0 + i, Z_GLU // (2 * WA))),
                              pl.BlockSpec((CONV_A_WIDTH, WA), lambda i: (0, 0)),
                              pl.BlockSpec((1, WA), lambda i: (0, 0)),
                              pl.BlockSpec((1, WA), lambda i: (0, 0)),
                              pl.BlockSpec((1, WA), lambda i: (0, 0))],
        out_specs=pl.BlockSpec((t, WA), lambda i: (blk0 + i, 0)),
        out_shape=jax.ShapeDtypeStruct((z.shape[0], WA), BF16),
        input_output_aliases=aliases,
        scratch_shapes=[pltpu.VMEM((nseg * (seg + 2 * CONV_PAD), WA), F32)],
        compiler_params=_params("parallel"),
        name=f"conv_a_seg{seg}",
    )(*pre_args, z, conv_w, conv_b.reshape(1, WA), ln_w.reshape(1, WA), ln_b.reshape(1, WA))


POOL_ROWS = 64
POOL_HALF_MAX = max(POOL_WINDOWS) // 2


def _pool_kernel(z_ref, o_ref, pad_ref, *, dil):
    t = z_ref.shape[0]
    npos = t // dil
    padr = POOL_HALF_MAX * dil
    zeros = jnp.zeros((padr, WB), F32)
    pad_ref[pl.ds(0, padr), :] = zeros
    pad_ref[pl.ds(padr, t), :] = z_ref[...]
    pad_ref[pl.ds(padr + t, padr), :] = zeros
    for c in range(t // POOL_ROWS):
        r0 = c * POOL_ROWS
        pos = (r0 + lax.broadcasted_iota(jnp.int32, (POOL_ROWS, POOL_GC), 0)) >> (dil.bit_length() - 1)
        for g, win in enumerate(POOL_WINDOWS):
            lanes = pl.ds(g * POOL_GC, POOL_GC)
            acc = None
            for o in range(-(win // 2), win - win // 2):
                term = pad_ref[pl.ds(padr + r0 + o * dil, POOL_ROWS), lanes]
                acc = term if acc is None else acc + term
            lo = jnp.maximum(pos - win // 2, 0)
            hi = jnp.minimum(pos - win // 2 + win, npos)
            cnt = (hi - lo).astype(F32)
            x = pad_ref[pl.ds(padr + r0, POOL_ROWS), lanes]
            o_ref[pl.ds(r0, POOL_ROWS), lanes] = (acc / cnt - x).astype(o_ref.dtype)


def _pool(z, row0, rows, t, dil, prev):
    blk0 = row0 // t
    pre_args, pre_specs, aliases = _into(prev)
    return pl.pallas_call(
        _skip_refs(functools.partial(_pool_kernel, dil=dil), len(pre_args)),
        grid=(rows // t,),
        in_specs=pre_specs + [pl.BlockSpec((t, WB), lambda i: (blk0 + i, Z_POOL // WB))],
        out_specs=pl.BlockSpec((t, WB), lambda i: (blk0 + i, 0)),
        out_shape=jax.ShapeDtypeStruct((z.shape[0], WB), BF16),
        input_output_aliases=aliases,
        scratch_shapes=[pltpu.VMEM((t + 2 * POOL_HALF_MAX * dil, WB), F32)],
        compiler_params=_params("parallel"),
        name=f"pool_dil{dil}",
    )(*pre_args, z)


SCAN_CHUNK = 256
NSTREAM = 2 * N_HEADS
N_COLVEC = 4


def _log_sigmoid(x):
    return jnp.minimum(x, 0.0) - jnp.log1p(jnp.exp(-jnp.abs(x)))


def _lane_scan(x, op, ident, reverse):
    lane = lax.broadcasted_iota(jnp.int32, x.shape, 1)
    n = x.shape[1]
    sh = 1
    while sh < n:
        if reverse:
            x = op(x, jnp.where(lane < n - sh, pltpu.roll(x, n - sh, 1), ident))
        else:
            x = op(x, jnp.where(lane >= sh, pltpu.roll(x, sh, 1), ident))
        sh *= 2
    return x


def _mlstm_kernel(*refs, zero_init, emit_state):
    q_ref, k_ref, v_ref, og_ref, g_ref, nw_ref = refs[:6]
    pos = 6
    if not zero_init:
        c0_ref, n0_ref, m0_ref = refs[pos:pos + 3]
        pos += 3
    o_ref = refs[pos]
    pos += 1
    if emit_state:
        co_ref, no_ref, mo_ref = refs[pos:pos + 3]
        pos += 3
    rowbuf, decbuf, colbuf, h_scr = refs[pos:pos + 4]
    c_scr = refs[pos + 4:pos + 4 + NSTREAM]
    n_scr = refs[pos + 4 + NSTREAM:pos + 4 + 2 * NSTREAM]

    t = SCAN_CHUNK
    nsteps = q_ref.shape[0] // t

    log_i = g_ref[0]
    log_f = _log_sigmoid(g_ref[1])
    backward = (lax.broadcasted_iota(jnp.int32, log_i.shape, 0) & N_HEADS) != 0

    def scan(x, op, ident):
        return jnp.where(backward, _lane_scan(x, op, ident, True), _lane_scan(x, op, ident, False))

    b = scan(log_f, jnp.add, 0.0)
    c = log_i - b
    c_run = scan(c, jnp.maximum, -jnp.inf)
    b_tot = jnp.sum(log_f, axis=1, keepdims=True)
    c_tot = jnp.max(c, axis=1, keepdims=True)
    m = jnp.zeros((NSTREAM, 1), F32) if zero_init else m0_ref[...]
    unused_rows = jnp.zeros((LANES - N_COLVEC * NSTREAM, t), F32)
    for i in range(nsteps):
        sl = slice(i * NSTREAM, (i + 1) * NSTREAM)
        m_last = jnp.maximum(m, c_tot[sl])
        m_run = jnp.maximum(m, c_run[sl])
        packed = [m_run, jnp.exp(m - m_run), jnp.exp(-(b[sl] + m_run)), jnp.exp(c[sl] - m_last), unused_rows]
        colbuf[i] = jnp.concatenate(packed, axis=0).T
        rowbuf[i] = c[sl]
        decbuf[i] = jnp.broadcast_to(jnp.exp(m - m_last), (NSTREAM, HEAD_DIM))
        m = b_tot[sl] + m_last
    if emit_state:
        mo_ref[...] = m

    if not zero_init:
        for s in range(NSTREAM):
            d, hd = divmod(s, N_HEADS)
            c_scr[s][...] = c0_ref[d, hd]
            n_scr[s][...] = n0_ref[s:s + 1, :]

    t_idx = lax.broadcasted_iota(jnp.int32, (t, t), 0)
    s_idx = lax.broadcasted_iota(jnp.int32, (t, t), 1)
    scanned = (s_idx <= t_idx, s_idx >= t_idx)

    def step(i, has_state, update_state):
        col = colbuf[i]
        for s in range(NSTREAM):
            d, hd = divmod(s, N_HEADS)
            cidx = i if d == 0 else nsteps - 1 - i
            rows = pl.ds(pl.multiple_of(cidx * t, t), t)
            cols = pl.ds(hd * HEAD_DIM, HEAD_DIM)
            m_run, w_int, floor, w_key = (col[:, vec * NSTREAM + s:vec * NSTREAM + s + 1]
                                          for vec in range(N_COLVEC))
            c_row = rowbuf[i, s:s + 1, :]
            dec = decbuf[i, s:s + 1, :]
            q = q_ref[rows, cols] * (HEAD_DIM ** -0.5)
            k = k_ref[rows, cols]
            v = v_ref[rows, cols]
            dmat = jnp.exp(jnp.where(scanned[d], c_row - m_run, -jnp.inf))
            sm = lax.dot_general(q, k, (((1,), (1,)), ((), ())), preferred_element_type=F32) * dmat
            num = jnp.dot(sm, v, preferred_element_type=F32)
            den = jnp.sum(sm, axis=1, keepdims=True)
            if has_state:
                c_prev = c_scr[s][...]
                n_prev = n_scr[s][...]
                num = num + w_int * jnp.dot(q, c_prev, preferred_element_type=F32)
                den = den + w_int * jnp.sum(q * n_prev, axis=1, keepdims=True)
            h_scr[d, rows, cols] = num * (1.0 / jnp.maximum(jnp.abs(den), floor))
            if update_state:
                wk = k * w_key
                c_new = lax.dot_general(wk, v, (((0,), (0,)), ((), ())), preferred_element_type=F32)
                n_new = jnp.sum(wk, axis=0, keepdims=True)
                if has_state:
                    c_new = c_new + dec * c_prev
                    n_new = n_new + dec * n_prev
                c_scr[s][...] = c_new
                n_scr[s][...] = n_new

    first = 1 if zero_init else 0
    last = nsteps if emit_state else nsteps - 1
    if zero_init:
        step(0, False, emit_state or nsteps > 1)
    if last - first == 1:
        step(first, True, True)
    elif last > first:
        def body(i, carry):
            step(i, True, True)
            return carry
        lax.fori_loop(first, last, body, 0)
    if not emit_state and not (zero_init and nsteps == 1):
        step(nsteps - 1, True, False)

    for hd in range(N_HEADS):
        cols = pl.ds(hd * HEAD_DIM, HEAD_DIM)
        hn = _rms(h_scr[0, :, cols] + h_scr[1, :, cols])
        o_ref[:, cols] = (hn * nw_ref[:, cols] * jax.nn.sigmoid(og_ref[:, cols])).astype(o_ref.dtype)
    if emit_state:
        for s in range(NSTREAM):
            d, hd = divmod(s, N_HEADS)
            co_ref[d, hd] = c_scr[s][...]
            no_ref[s:s + 1, :] = n_scr[s][...]


def _mlstm(z, gates, norm_w, row0, nb, seq, init_state, layer, prev_hc, state_out):
    m_tot = z.shape[0]
    t = SCAN_CHUNK
    nsteps = seq // t
    blk0 = row0 // seq
    nrow = nsteps * NSTREAM
    g = gates.reshape(nb, nsteps, t, 2, 2, N_HEADS).transpose(0, 4, 3, 1, 5, 2)
    g = jnp.concatenate([g[:, :, 0], g[:, :, 1, ::-1]], axis=3)
    g = g.reshape(nb, 2, nrow, t)

    big = seq * WC * 4 > (2 << 20)

    def zspec(off):
        idx = lambda b: (blk0 + b, off // WC)
        return pl.BlockSpec((seq, WC), idx, pipeline_mode=pl.Buffered(1)) if big else pl.BlockSpec((seq, WC), idx)

    in_specs = [zspec(Z_QKV), zspec(Z_QKV + WC), zspec(Z_QKV + 2 * WC), zspec(Z_OG),
                pl.BlockSpec((None, 2, nrow, t), lambda b: (b, 0, 0, 0)),
                pl.BlockSpec((1, WC), lambda b: (0, 0))]
    args = [z, z, z, z, g, norm_w.reshape(1, WC)]
    if init_state is not None:
        sc, sn, sm = init_state
        depth = sc.shape[1]
        in_specs += [pl.BlockSpec((None, None, 2, N_HEADS, HEAD_DIM, HEAD_DIM), lambda b: (b, layer, 0, 0, 0, 0)),
                     pl.BlockSpec((None, None, NSTREAM, HEAD_DIM), lambda b: (b, layer, 0, 0)),
                     pl.BlockSpec((None, None, NSTREAM, 1), lambda b: (b, layer, 0, 0))]
        args += [sc, sn.reshape(nb, depth, NSTREAM, HEAD_DIM), sm.reshape(nb, depth, NSTREAM, 1)]
    out_specs = [pl.BlockSpec((seq, WC), lambda b: (blk0 + b, 0))]
    out_shape = [jax.ShapeDtypeStruct((m_tot, WC), BF16)]
    aliases = {}
    pre_args, pre_specs = [], []
    if prev_hc is not None:
        pre_args.append(prev_hc)
        pre_specs.append(pl.BlockSpec(memory_space=pl.ANY))
        aliases[0] = 0
    if state_out is not None:
        depth, prev_c = state_out
        out_specs += [pl.BlockSpec((None, None, 2, N_HEADS, HEAD_DIM, HEAD_DIM), lambda b: (b, layer, 0, 0, 0, 0)),
                      pl.BlockSpec((None, NSTREAM, HEAD_DIM), lambda b: (b, 0, 0)),
                      pl.BlockSpec((None, NSTREAM, 1), lambda b: (b, 0, 0))]
        out_shape += [jax.ShapeDtypeStruct((nb, depth, 2, N_HEADS, HEAD_DIM, HEAD_DIM), F32),
                      jax.ShapeDtypeStruct((nb, NSTREAM, HEAD_DIM), F32),
                      jax.ShapeDtypeStruct((nb, NSTREAM, 1), F32)]
        if prev_c is not None:
            aliases[len(pre_args)] = 1
            pre_args.append(prev_c)
            pre_specs.append(pl.BlockSpec(memory_space=pl.ANY))
    body = functools.partial(_mlstm_kernel, zero_init=init_state is None, emit_state=state_out is not None)
    return pl.pallas_call(
        _skip_refs(body, len(pre_args)),
        grid=(nb,),
        in_specs=pre_specs + in_specs,
        out_specs=out_specs,
        out_shape=out_shape,
        input_output_aliases=aliases,
        scratch_shapes=[pltpu.VMEM((nsteps, NSTREAM, t), F32),
                        pltpu.VMEM((nsteps, NSTREAM, HEAD_DIM), F32),
                        pltpu.VMEM((nsteps, t, LANES), F32),
                        pltpu.VMEM((2, seq, WC), F32)]
        + [pltpu.VMEM((HEAD_DIM, HEAD_DIM), F32)] * NSTREAM + [pltpu.VMEM((1, HEAD_DIM), F32)] * NSTREAM,
        compiler_params=_params("parallel"),
        name=f"mlstm_seq{seq}",
    )(*pre_args, *args)


def _merge_kernel(h_ref, a_ref, p_ref, c_ref, wga_ref, wgb_ref, wgc_ref, bga_ref, bgb_ref, bgc_ref,
                  wa_ref, wp_ref, ps_ref, wc_ref, o_ref):
    h = h_ref[...]

    def gate(w_ref, b_ref):
        return jax.nn.sigmoid(jnp.dot(h, w_ref[...], preferred_element_type=F32) + b_ref[...])

    ya = jnp.dot(a_ref[...], wa_ref[...], preferred_element_type=F32)
    yb = jnp.dot(p_ref[...], wp_ref[...], preferred_element_type=F32) * ps_ref[...]
    yc = jnp.dot(c_ref[...], wc_ref[...], preferred_element_type=F32)
    mix = gate(wga_ref, bga_ref) * ya + gate(wgb_ref, bgb_ref) * yb + gate(wgc_ref, bgc_ref) * yc
    o_ref[...] = mix.astype(o_ref.dtype)


def _merge(h, a_act, p, hc, w_merge, b_merge, w_a_out, w_pool, pool_scale, w_c_out):
    m, d = h.shape
    tm = 512
    tn = d // N_POOL_GROUPS
    nj = d // tn

    def wg(k):
        return pl.BlockSpec((d, tn), lambda i, j: (0, k * nj + j))

    def bg(k):
        return pl.BlockSpec((1, tn), lambda i, j: (0, k * nj + j))

    return pl.pallas_call(
        _merge_kernel,
        grid=(m // tm, nj),
        in_specs=[pl.BlockSpec((tm, d), lambda i, j: (i, 0)),
                  pl.BlockSpec((tm, WA), lambda i, j: (i, 0)),
                  pl.BlockSpec((tm, POOL_GC), lambda i, j: (i, j)),
                  pl.BlockSpec((tm, WC), lambda i, j: (i, 0)),
                  wg(0), wg(1), wg(2), bg(0), bg(1), bg(2),
                  pl.BlockSpec((WA, tn), lambda i, j: (0, j)),
                  pl.BlockSpec((None, POOL_GC, tn), lambda i, j: (j, 0, 0)),
                  pl.BlockSpec((1, tn), lambda i, j: (0, j)),
                  pl.BlockSpec((WC, tn), lambda i, j: (0, j))],
        out_specs=pl.BlockSpec((tm, tn), lambda i, j: (i, j)),
        out_shape=jax.ShapeDtypeStruct((m, d), BF16),
        compiler_params=_params("parallel", "arbitrary"),
        name="merge",
    )(h, a_act, p, hc, w_merge, w_merge, w_merge, b_merge, b_merge, b_merge,
      w_a_out, w_pool, pool_scale.reshape(1, d), w_c_out)


def _proj_residual_kernel(a_ref, w_ref, x_ref, wpost_ref, gate_ref, wpre_ref, sc_ref, sh_ref, xo_ref, ho_ref,
                          *, sub):
    for r in range(a_ref.shape[0] // sub):
        rows = pl.ds(r * sub, sub)
        y = jnp.dot(a_ref[rows, :], w_ref[...], preferred_element_type=F32)
        xn = x_ref[rows, :] + gate_ref[...] * (_rms(y) * wpost_ref[...])
        xo_ref[rows, :] = xn
        hn = _rms(xn) * wpre_ref[...]
        ho_ref[rows, :] = (hn * (1.0 + sc_ref[...]) + sh_ref[...]).astype(ho_ref.dtype)


def _proj_residual(lay, a, w, x, w_post, w_pre_next, mod, gate, scale, shift, tm, sub, name):
    d = lay.d
    kdim = a.shape[1]
    row = pl.BlockSpec((1, d), lambda i: (0, 0))
    return pl.pallas_call(
        functools.partial(_proj_residual_kernel, sub=sub),
        grid=(lay.m // tm,),
        in_specs=[pl.BlockSpec((tm, kdim), lambda i: (i, 0)),
                  pl.BlockSpec((kdim, d), lambda i: (0, 0), pipeline_mode=pl.Buffered(1)),
                  pl.BlockSpec((tm, d), lambda i: (i, 0)),
                  row, lay.mod_spec(*gate, tm),
                  row, lay.mod_spec(*scale, tm), lay.mod_spec(*shift, tm)],
        out_specs=[pl.BlockSpec((tm, d), lambda i: (i, 0)), pl.BlockSpec((tm, d), lambda i: (i, 0))],
        out_shape=[jax.ShapeDtypeStruct((lay.m, d), F32), jax.ShapeDtypeStruct((lay.m, d), BF16)],
        compiler_params=_params("parallel"),
        name=name,
    )(a, w, x, w_post.reshape(1, d), mod, w_pre_next.reshape(1, d), mod, mod)


def _gelu_tanh(x):
    return 0.5 * x * (1.0 + jnp.tanh(0.7978845608028654 * (x + 0.044715 * (x * x * x))))


MXU_WIDTH = 256


def _ffn_up_kernel(h_ref, wu_ref, wg_ref, cw_ref, cb_ref, o_ref, *, shift, seg):
    h = h_ref[...]
    tm = h.shape[0]
    for c in range(o_ref.shape[1] // MXU_WIDTH):
        cols = pl.ds(c * MXU_WIDTH, MXU_WIDTH)
        u = jnp.dot(h, wu_ref[:, cols], preferred_element_type=F32)
        g = jnp.dot(h, wg_ref[:, cols], preferred_element_type=F32)
        if shift % SUBLANES == 0 and seg == tm:
            zeros = jnp.zeros((shift, MXU_WIDTH), F32)
            prev = jnp.concatenate([zeros, g[:tm - shift]], axis=0)
            nxt = jnp.concatenate([g[shift:], zeros], axis=0)
        else:
            r = lax.broadcasted_iota(jnp.int32, g.shape, 0) & (seg - 1)
            prev = jnp.where(r < shift, 0.0, pltpu.roll(g, shift, 0))
            nxt = jnp.where(r >= seg - shift, 0.0, pltpu.roll(g, tm - shift, 0))
        gc = cw_ref[0:1, cols] * prev + cw_ref[1:2, cols] * g + cw_ref[2:3, cols] * nxt + cb_ref[:, cols]
        o_ref[:, cols] = (_gelu_tanh(gc) * u).astype(o_ref.dtype)


def _ffn_up(h2, row0, rows, tm, shift, seg, w_up, conv_w, conv_b, prev):
    m, d = h2.shape
    dff = w_up.shape[1] // 2
    tn = 512
    nj = dff // tn
    blk0 = row0 // tm
    pre_args, pre_specs, aliases = _into(prev)
    return pl.pallas_call(
        _skip_refs(functools.partial(_ffn_up_kernel, shift=shift, seg=seg), len(pre_args)),
        grid=(rows // tm, nj),
        in_specs=pre_specs + [pl.BlockSpec((tm, d), lambda i, j: (blk0 + i, 0)),
                              pl.BlockSpec((d, tn), lambda i, j: (0, j)),
                              pl.BlockSpec((d, tn), lambda i, j: (0, nj + j)),
                              pl.BlockSpec((3, tn), lambda i, j: (0, j)),
                              pl.BlockSpec((1, tn), lambda i, j: (0, j))],
        out_specs=pl.BlockSpec((tm, tn), lambda i, j: (blk0 + i, j)),
        out_shape=jax.ShapeDtypeStruct((m, dff), BF16),
        input_output_aliases=aliases,
        compiler_params=_params("parallel", "arbitrary"),
        name=f"ffn_up_shift{shift}",
    )(*pre_args, h2, w_up, w_up, conv_w, conv_b.reshape(1, dff))


def kernel(x_prompt, x_sample, state_C, state_n, state_m, c, c_ctx, w_ada, b_ada, norm_mix_pre, norm_mix_post, norm_ffn_pre, norm_ffn_post, w_in, b_in, conv_a_w, conv_a_b, ln_a_w, ln_a_b, w_a_out, w_pool, pool_scale, mlstm_norm_w, w_c_out, w_out, w_ffn_up, ffn_conv_w, ffn_conv_b, w_ffn_down):
    bp, lp, d = x_prompt.shape
    bs, ls, _ = x_sample.shape
    depth = w_ada.shape[0]
    lay = _Layout(bp, lp, bs, ls, d)
    assert bs < N_COND_ROWS and lay.mp % ls == 0 and ls % lp == 0
    assert lp & (lp - 1) == 0 and GRID_W & (GRID_W - 1) == 0
    assert lp % SCAN_CHUNK == 0 and ls % SCAN_CHUNK == 0 and N_COLVEC * NSTREAM <= LANES

    cond = jnp.zeros((N_COND_ROWS, d), F32).at[:bs].set(c).at[bs].set(c_ctx)
    mod = _ada(cond, w_ada, b_ada).reshape(depth * N_COND_ROWS * N_MOD, 1, d)

    w_main = jnp.concatenate([w_in[:, :, OFF_QKV:OFF_GATES], w_in[:, :, :OFF_QKV]], axis=-1).astype(BF16)
    b_main = jnp.concatenate([b_in[:, OFF_QKV:OFF_GATES], b_in[:, :OFF_QKV]], axis=-1)
    w_gate = jnp.pad(w_in[:, :, OFF_GATES:OFF_MERGE], ((0, 0), (0, 0), (0, LANES - 4 * N_HEADS))).astype(BF16)
    b_gate = jnp.pad(b_in[:, OFF_GATES:OFF_MERGE], ((0, 0), (0, LANES - 4 * N_HEADS)))
    w_merge = w_in[:, :, OFF_MERGE:].astype(BF16)
    w_a_out_b = w_a_out.astype(BF16)
    w_pool_b = w_pool.astype(BF16)
    w_c_out_b = w_c_out.astype(BF16)
    w_out_b = w_out.astype(BF16)
    w_up_b = w_ffn_up.astype(BF16)
    w_down_b = w_ffn_down.astype(BF16)

    x = jnp.concatenate([x_prompt.reshape(lay.mp, d), x_sample.reshape(lay.ms, d)], axis=0)
    h = _norm_mod(lay, x, norm_mix_pre[0], mod, 0, 1, 0)
    new_c = None
    new_n, new_m = [], []
    for l in range(depth):
        z = _matmul_bias(h, w_main[l], b_main[l], 1024, 512, "in_proj")
        zg = _matmul_bias(h, w_gate[l], b_gate[l], 1024, LANES, "gate_proj")[:, :4 * N_HEADS]

        conv_args = (conv_a_w[l], conv_a_b[l], ln_a_w[l], ln_a_b[l])
        a_act = _conv_a(z, 0, lay.mp, lp, *conv_args, None)
        a_act = _conv_a(z, lay.mp, lay.ms, GRID_W, *conv_args, a_act)
        p = _pool(z, 0, lay.mp, lp, 1, None)
        p = _pool(z, lay.mp, lay.ms, ls, GRID_W, p)
        hc, new_c, n_l, m_l = _mlstm(z, zg[:lay.mp], mlstm_norm_w[l], 0, bp, lp, None, l, None, (depth, new_c))
        (hc,) = _mlstm(z, zg[lay.mp:], mlstm_norm_w[l], lay.mp, bs, ls, (state_C, state_n, state_m), l, hc, None)
        new_n.append(n_l.reshape(bp, 2, N_HEADS, HEAD_DIM))
        new_m.append(m_l.reshape(bp, 2, N_HEADS))

        mix_in = _merge(h, a_act, p, hc, w_merge[l], b_in[l, OFF_MERGE:].reshape(1, -1),
                        w_a_out_b[l], w_pool_b[l], pool_scale[l], w_c_out_b[l])
        x, h2 = _proj_residual(lay, mix_in, w_out_b[l], x, norm_mix_post[l], norm_ffn_pre[l], mod,
                               (l, 2), (l, 4), (l, 3), 512, 256, "out_proj")
        ffn_args = (w_up_b[l], ffn_conv_w[l], ffn_conv_b[l])
        f_in = _ffn_up(h2, 0, lay.mp, ls, 1, lp, *ffn_args, None)
        f_in = _ffn_up(h2, lay.mp, lay.ms, ls, GRID_W, ls, *ffn_args, f_in)
        nl = min(l + 1, depth - 1)
        x, h = _proj_residual(lay, f_in, w_down_b[l], x, norm_ffn_post[l], norm_mix_pre[nl], mod,
                              (l, 5), (nl, 1), (nl, 0), 256, 128, "ffn_down")

    y_prompt = x[:lay.mp].reshape(bp, lp, d)
    y_sample = x[lay.mp:].reshape(bs, ls, d)
    return (y_prompt, y_sample, new_c, jnp.stack(new_n, axis=1), jnp.stack(new_m, axis=1))
```

```python
import functools

import jax
import jax.numpy as jnp
from jax import lax
from jax.experimental import pallas as pl
from jax.experimental.pallas import tpu as pltpu

GRID_W = 64
WA = 512
WB = 512
N_POOL_GROUPS = 4
POOL_GC = WB // N_POOL_GROUPS
POOL_WINDOWS = (2, 4, 8, 16)
WC = 1024
N_HEADS = 4
HEAD_DIM = WC // N_HEADS
CONV_A_WIDTH = 31
CHUNK = 64
N_BRANCH = 3
N_MOD = 6
EPS = 1e-6

OFF_POOL = 2 * WA
OFF_QKV = OFF_POOL + WB
OFF_OG = OFF_QKV + 3 * WC
OFF_GATES = OFF_OG + WC
OFF_MERGE = OFF_GATES + 4 * N_HEADS

Z_QKV = 0
Z_OG = Z_QKV + 3 * WC
Z_GLU = Z_OG + WC
Z_POOL = Z_GLU + 2 * WA
Z_WIDTH = Z_POOL + WB

LANES = 128
SUBLANES = 8
VMEM_LIMIT_BYTES = 56 * 1024 * 1024

N_COND_ROWS = 8
BF16 = jnp.bfloat16
F32 = jnp.float32


def _params(*sem):
    return pltpu.CompilerParams(dimension_semantics=sem, vmem_limit_bytes=VMEM_LIMIT_BYTES)


def _rms(x):
    return x * lax.rsqrt(jnp.mean(x * x, axis=-1, keepdims=True) + EPS)


def _ada_kernel(c_ref, w_ref, b_ref, o_ref):
    c = c_ref[...]
    a = c * jax.nn.sigmoid(c)
    o_ref[...] = jnp.dot(a, w_ref[...], preferred_element_type=F32) + b_ref[...]


def _ada(cond, w_ada, b_ada):
    depth, d, n = w_ada.shape
    tn = 1024
    return pl.pallas_call(
        _ada_kernel,
        grid=(depth, n // tn),
        in_specs=[pl.BlockSpec((N_COND_ROWS, d), lambda l, j: (0, 0)),
                  pl.BlockSpec((None, d, tn), lambda l, j: (l, 0, j)),
                  pl.BlockSpec((None, 1, tn), lambda l, j: (l, 0, j))],
        out_specs=pl.BlockSpec((None, N_COND_ROWS, tn), lambda l, j: (l, 0, j)),
        out_shape=jax.ShapeDtypeStruct((depth, N_COND_ROWS, n), F32),
        compiler_params=_params("arbitrary", "arbitrary"),
        name="ada",
    )(cond, w_ada, b_ada.reshape(depth, 1, n))


class _Layout:
    def __init__(self, bp, lp, bs, ls, d):
        self.bp, self.lp, self.bs, self.ls, self.d = bp, lp, bs, ls, d
        self.mp = bp * lp
        self.ms = bs * ls
        self.m = self.mp + self.ms

    def mod_spec(self, layer, k, tm):
        mp, ls, bs = self.mp, self.ls, self.bs

        def index(i, *_):
            start = i * tm
            row = jnp.where(start < mp, bs, (start - mp) // ls)
            return ((layer * N_COND_ROWS + row) * N_MOD + k, 0, 0)

        return pl.BlockSpec((None, 1, self.d), index)


def _norm_mod_kernel(x_ref, w_ref, sc_ref, sh_ref, o_ref):
    h = _rms(x_ref[...]) * w_ref[...]
    o_ref[...] = (h * (1.0 + sc_ref[...]) + sh_ref[...]).astype(o_ref.dtype)


def _norm_mod(lay, x, w, mod, layer, k_scale, k_shift):
    tm = 512
    d = lay.d
    return pl.pallas_call(
        _norm_mod_kernel,
        grid=(lay.m // tm,),
        in_specs=[pl.BlockSpec((tm, d), lambda i: (i, 0)),
                  pl.BlockSpec((1, d), lambda i: (0, 0)),
                  lay.mod_spec(layer, k_scale, tm),
                  lay.mod_spec(layer, k_shift, tm)],
        out_specs=pl.BlockSpec((tm, d), lambda i: (i, 0)),
        out_shape=jax.ShapeDtypeStruct((lay.m, d), BF16),
        compiler_params=_params("parallel"),
        name="norm_mod",
    )(x, w.reshape(1, d), mod, mod)


def _matmul_bias_kernel(a_ref, w_ref, b_ref, o_ref):
    o_ref[...] = jnp.dot(a_ref[...], w_ref[...], preferred_element_type=F32) + b_ref[...]


def _matmul_bias(a, w, b, layer, col0, n, tm, tn, name):
    m, k = a.shape
    c0 = col0 // tn
    return pl.pallas_call(
        _matmul_bias_kernel,
        grid=(m // tm, n // tn),
        in_specs=[pl.BlockSpec((tm, k), lambda i, j: (i, 0)),
                  pl.BlockSpec((None, k, tn), lambda i, j: (layer, 0, c0 + j)),
                  pl.BlockSpec((None, 1, tn), lambda i, j: (layer, 0, c0 + j))],
        out_specs=pl.BlockSpec((tm, tn), lambda i, j: (i, j)),
        out_shape=jax.ShapeDtypeStruct((m, n), F32),
        compiler_params=_params("parallel", "arbitrary"),
        name=name,
    )(a, w, b)


def _into(prev):
    if prev is None:
        return [], [], {}
    return [prev], [pl.BlockSpec(memory_space=pl.ANY)], {0: 0}


def _skip_refs(kernel_fn, n):
    if n == 0:
        return kernel_fn
    return lambda *refs: kernel_fn(*refs[n:])


CONV_PAD = 16
CONV_ROWS = 32


def _conv_a_kernel(z_ref, w_ref, b_ref, lw_ref, lb_ref, o_ref, pad_ref, shifted_ref, *, seg):
    t = z_ref.shape[0]
    stride = seg + 2 * CONV_PAD
    half = CONV_A_WIDTH // 2
    zeros = jnp.zeros((CONV_PAD, WA), F32)
    for s in range(t // seg):
        base = s * stride
        z = z_ref[pl.ds(s * seg, seg), :]
        pad_ref[pl.ds(base, CONV_PAD), :] = zeros
        pad_ref[pl.ds(base + CONV_PAD, seg), :] = z[:, :WA] * jax.nn.sigmoid(z[:, WA:])
        pad_ref[pl.ds(base + CONV_PAD + seg, CONV_PAD), :] = zeros
    nshift = pad_ref.shape[0] - SUBLANES
    for rho in range(1, SUBLANES):
        shifted_ref[rho - 1, pl.ds(0, nshift), :] = pad_ref[pl.ds(rho, nshift), :]
    for s in range(t // seg):
        for c in range(seg // CONV_ROWS):
            row0 = s * stride + CONV_PAD + c * CONV_ROWS
            acc = jnp.broadcast_to(b_ref[...], (CONV_ROWS, WA))
            for tap in range(CONV_A_WIDTH):
                rho = (tap - half) % SUBLANES
                rows = pl.ds(row0 + tap - half - rho, CONV_ROWS)
                window = pad_ref[rows, :] if rho == 0 else shifted_ref[rho - 1, rows, :]
                acc = acc + window * w_ref[pl.ds(tap, 1), :]
            mu = jnp.mean(acc, axis=-1, keepdims=True)
            cen = acc - mu
            var = jnp.mean(cen * cen, axis=-1, keepdims=True)
            y = cen * lax.rsqrt(var + EPS) * lw_ref[...] + lb_ref[...]
            y = y * jax.nn.sigmoid(y)
            o_ref[pl.ds(s * seg + c * CONV_ROWS, CONV_ROWS), :] = y.astype(o_ref.dtype)


def _conv_a(z, row0, rows, seg, conv_w, conv_b, ln_w, ln_b, prev):
    t = 256
    nseg = t // seg
    blk0 = row0 // t
    pre_args, pre_specs, aliases = _into(prev)
    return pl.pallas_call(
        _skip_refs(functools.partial(_conv_a_kernel, seg=seg), len(pre_args)),
        grid=(rows // t,),
        in_specs=pre_specs + [pl.BlockSpec((t, 2 * WA), lambda i: (blk0 + i, Z_GLU // (2 * WA))),
                              pl.BlockSpec((CONV_A_WIDTH, WA), lambda i: (0, 0)),
                              pl.BlockSpec((1, WA), lambda i: (0, 0)),
                              pl.BlockSpec((1, WA), lambda i: (0, 0)),
                              pl.BlockSpec((1, WA), lambda i: (0, 0))],
        out_specs=pl.BlockSpec((t, WA), lambda i: (blk0 + i, 0)),
        out_shape=jax.ShapeDtypeStruct((z.shape[0], WA), BF16),
        input_output_aliases=aliases,
        scratch_shapes=[pltpu.VMEM((nseg * (seg + 2 * CONV_PAD), WA), F32),
                        pltpu.VMEM((SUBLANES - 1, nseg * (seg + 2 * CONV_PAD), WA), F32)],
        compiler_params=_params("parallel"),
        name=f"conv_a_seg{seg}",
    )(*pre_args, z, conv_w, conv_b.reshape(1, WA), ln_w.reshape(1, WA), ln_b.reshape(1, WA))


POOL_ROWS = 64
POOL_HALF_MAX = max(POOL_WINDOWS) // 2


def _pool_kernel(z_ref, o_ref, pad_ref, *, dil):
    t = z_ref.shape[0]
    npos = t // dil
    padr = POOL_HALF_MAX * dil
    zeros = jnp.zeros((padr, WB), F32)
    pad_ref[pl.ds(0, padr), :] = zeros
    pad_ref[pl.ds(padr, t), :] = z_ref[...]
    pad_ref[pl.ds(padr + t, padr), :] = zeros
    for c in range(t // POOL_ROWS):
        r0 = c * POOL_ROWS
        pos = (r0 + lax.broadcasted_iota(jnp.int32, (POOL_ROWS, POOL_GC), 0)) >> (dil.bit_length() - 1)
        for g, win in enumerate(POOL_WINDOWS):
            lanes = pl.ds(g * POOL_GC, POOL_GC)
            acc = None
            for o in range(-(win // 2), win - win // 2):
                term = pad_ref[pl.ds(padr + r0 + o * dil, POOL_ROWS), lanes]
                acc = term if acc is None else acc + term
            lo = jnp.maximum(pos - win // 2, 0)
            hi = jnp.minimum(pos - win // 2 + win, npos)
            cnt = (hi - lo).astype(F32)
            x = pad_ref[pl.ds(padr + r0, POOL_ROWS), lanes]
            o_ref[pl.ds(r0, POOL_ROWS), lanes] = (acc / cnt - x).astype(o_ref.dtype)


def _pool(z, row0, rows, t, dil, prev):
    blk0 = row0 // t
    pre_args, pre_specs, aliases = _into(prev)
    return pl.pallas_call(
        _skip_refs(functools.partial(_pool_kernel, dil=dil), len(pre_args)),
        grid=(rows // t,),
        in_specs=pre_specs + [pl.BlockSpec((t, WB), lambda i: (blk0 + i, Z_POOL // WB))],
        out_specs=pl.BlockSpec((t, WB), lambda i: (blk0 + i, 0)),
        out_shape=jax.ShapeDtypeStruct((z.shape[0], WB), BF16),
        input_output_aliases=aliases,
        scratch_shapes=[pltpu.VMEM((t + 2 * POOL_HALF_MAX * dil, WB), F32)],
        compiler_params=_params("parallel"),
        name=f"pool_dil{dil}",
    )(*pre_args, z)


SCAN_CHUNK = 256
NSTREAM = 2 * N_HEADS
N_COLVEC = 4


def _log_sigmoid(x):
    return jnp.minimum(x, 0.0) - jnp.log1p(jnp.exp(-jnp.abs(x)))


def _lane_scan(x, op, ident, reverse):
    lane = lax.broadcasted_iota(jnp.int32, x.shape, 1)
    n = x.shape[1]
    sh = 1
    while sh < n:
        if reverse:
            x = op(x, jnp.where(lane < n - sh, pltpu.roll(x, n - sh, 1), ident))
        else:
            x = op(x, jnp.where(lane >= sh, pltpu.roll(x, sh, 1), ident))
        sh *= 2
    return x


def _mlstm_kernel(*refs, zero_init, emit_state):
    q_ref, k_ref, v_ref, og_ref, g_ref, nw_ref = refs[:6]
    pos = 6
    if not zero_init:
        c0_ref, n0_ref, m0_ref = refs[pos:pos + 3]
        pos += 3
    o_ref = refs[pos]
    pos += 1
    if emit_state:
        co_ref, no_ref, mo_ref = refs[pos:pos + 3]
        pos += 3
    rowbuf, decbuf, colbuf, h_scr = refs[pos:pos + 4]
    c_scr = refs[pos + 4:pos + 4 + NSTREAM]
    n_scr = refs[pos + 4 + NSTREAM:pos + 4 + 2 * NSTREAM]

    t = SCAN_CHUNK
    nsteps = q_ref.shape[0] // t

    log_i = g_ref[0]
    log_f = _log_sigmoid(g_ref[1])
    backward = (lax.broadcasted_iota(jnp.int32, log_i.shape, 0) & N_HEADS) != 0

    def scan(x, op, ident):
        return jnp.where(backward, _lane_scan(x, op, ident, True), _lane_scan(x, op, ident, False))

    b = scan(log_f, jnp.add, 0.0)
    c = log_i - b
    c_run = scan(c, jnp.maximum, -jnp.inf)
    b_tot = jnp.sum(log_f, axis=1, keepdims=True)
    c_tot = jnp.max(c, axis=1, keepdims=True)
    m = jnp.zeros((NSTREAM, 1), F32) if zero_init else m0_ref[...]
    unused_rows = jnp.zeros((LANES - N_COLVEC * NSTREAM, t), F32)
    for i in range(nsteps):
        sl = slice(i * NSTREAM, (i + 1) * NSTREAM)
        m_last = jnp.maximum(m, c_tot[sl])
        m_run = jnp.maximum(m, c_run[sl])
        packed = [m_run, jnp.exp(m - m_run), jnp.exp(-(b[sl] + m_run)), jnp.exp(c[sl] - m_last), unused_rows]
        colbuf[i] = jnp.concatenate(packed, axis=0).T
        rowbuf[i] = c[sl]
        decbuf[i] = jnp.broadcast_to(jnp.exp(m - m_last), (NSTREAM, HEAD_DIM))
        m = b_tot[sl] + m_last
    if emit_state:
        mo_ref[...] = m

    if not zero_init:
        for s in range(NSTREAM):
            d, hd = divmod(s, N_HEADS)
            c_scr[s][...] = c0_ref[d, hd]
            n_scr[s][...] = n0_ref[s:s + 1, :]

    t_idx = lax.broadcasted_iota(jnp.int32, (t, t), 0)
    s_idx = lax.broadcasted_iota(jnp.int32, (t, t), 1)
    scanned = (s_idx <= t_idx, s_idx >= t_idx)

    def step(i, has_state, update_state):
        col = colbuf[i]

        def colvec(vec, s):
            return col[:, vec * NSTREAM + s:vec * NSTREAM + s + 1]

        def where(s):
            d, hd = divmod(s, N_HEADS)
            cidx = i if d == 0 else nsteps - 1 - i
            return d, pl.ds(pl.multiple_of(cidx * t, t), t), pl.ds(hd * HEAD_DIM, HEAD_DIM)

        scores = []
        for s in range(NSTREAM):
            d, rows, cols = where(s)
            dmat = jnp.exp(jnp.where(scanned[d], rowbuf[i, s:s + 1, :] - colvec(0, s), -jnp.inf))
            q = q_ref[rows, cols] * (HEAD_DIM ** -0.5)
            scores.append(lax.dot_general(q, k_ref[rows, cols], (((1,), (1,)), ((), ())),
                                          preferred_element_type=F32) * dmat)
        for s in range(NSTREAM):
            d, rows, cols = where(s)
            num = jnp.dot(scores[s], v_ref[rows, cols], preferred_element_type=F32)
            den = jnp.sum(scores[s], axis=1, keepdims=True)
            if has_state:
                q = q_ref[rows, cols] * (HEAD_DIM ** -0.5)
                w_int = colvec(1, s)
                num = num + w_int * jnp.dot(q, c_scr[s][...], preferred_element_type=F32)
                den = den + w_int * jnp.sum(q * n_scr[s][...], axis=1, keepdims=True)
            h_scr[d, rows, cols] = num * (1.0 / jnp.maximum(jnp.abs(den), colvec(2, s)))
        if update_state:
            for s in range(NSTREAM):
                d, rows, cols = where(s)
                wk = k_ref[rows, cols] * colvec(3, s)
                c_new = lax.dot_general(wk, v_ref[rows, cols], (((0,), (0,)), ((), ())),
                                        preferred_element_type=F32)
                n_new = jnp.sum(wk, axis=0, keepdims=True)
                if has_state:
                    dec = decbuf[i, s:s + 1, :]
                    c_new = c_new + dec * c_scr[s][...]
                    n_new = n_new + dec * n_scr[s][...]
                c_scr[s][...] = c_new
                n_scr[s][...] = n_new

    first = 1 if zero_init else 0
    last = nsteps if emit_state else nsteps - 1
    if zero_init:
        step(0, False, emit_state or nsteps > 1)
    if last - first == 1:
        step(first, True, True)
    elif last > first:
        def body(i, carry):
            step(i, True, True)
            return carry
        lax.fori_loop(first, last, body, 0)
    if not emit_state and not (zero_init and nsteps == 1):
        step(nsteps - 1, True, False)

    for hd in range(N_HEADS):
        cols = pl.ds(hd * HEAD_DIM, HEAD_DIM)
        hn = _rms(h_scr[0, :, cols] + h_scr[1, :, cols])
        o_ref[:, cols] = (hn * nw_ref[:, cols] * jax.nn.sigmoid(og_ref[:, cols])).astype(o_ref.dtype)
    if emit_state:
        for s in range(NSTREAM):
            d, hd = divmod(s, N_HEADS)
            co_ref[d, hd] = c_scr[s][...]
            no_ref[s:s + 1, :] = n_scr[s][...]


def _mlstm(z, gates, norm_w, row0, nb, seq, init_state, layer, prev_hc, state_out):
    m_tot = z.shape[0]
    t = SCAN_CHUNK
    nsteps = seq // t
    blk0 = row0 // seq
    nrow = nsteps * NSTREAM
    g = gates.reshape(nb, nsteps, t, 2, 2, N_HEADS).transpose(0, 4, 3, 1, 5, 2)
    g = jnp.concatenate([g[:, :, 0], g[:, :, 1, ::-1]], axis=3)
    g = g.reshape(nb, 2, nrow, t)

    big = seq * WC * 4 > (2 << 20)

    def zspec(off):
        idx = lambda b: (blk0 + b, off // WC)
        return pl.BlockSpec((seq, WC), idx, pipeline_mode=pl.Buffered(1)) if big else pl.BlockSpec((seq, WC), idx)

    in_specs = [zspec(Z_QKV), zspec(Z_QKV + WC), zspec(Z_QKV + 2 * WC), zspec(Z_OG),
                pl.BlockSpec((None, 2, nrow, t), lambda b: (b, 0, 0, 0)),
                pl.BlockSpec((1, WC), lambda b: (0, 0))]
    args = [z, z, z, z, g, norm_w.reshape(1, WC)]
    if init_state is not None:
        sc, sn, sm = init_state
        depth = sc.shape[1]
        in_specs += [pl.BlockSpec((None, None, 2, N_HEADS, HEAD_DIM, HEAD_DIM), lambda b: (b, layer, 0, 0, 0, 0)),
                     pl.BlockSpec((None, None, NSTREAM, HEAD_DIM), lambda b: (b, layer, 0, 0)),
                     pl.BlockSpec((None, None, NSTREAM, 1), lambda b: (b, layer, 0, 0))]
        args += [sc, sn.reshape(nb, depth, NSTREAM, HEAD_DIM), sm.reshape(nb, depth, NSTREAM, 1)]
    out_specs = [pl.BlockSpec((seq, WC), lambda b: (blk0 + b, 0))]
    out_shape = [jax.ShapeDtypeStruct((m_tot, WC), BF16)]
    aliases = {}
    pre_args, pre_specs = [], []
    if prev_hc is not None:
        pre_args.append(prev_hc)
        pre_specs.append(pl.BlockSpec(memory_space=pl.ANY))
        aliases[0] = 0
    if state_out is not None:
        depth, prev_c = state_out
        out_specs += [pl.BlockSpec((None, None, 2, N_HEADS, HEAD_DIM, HEAD_DIM), lambda b: (b, layer, 0, 0, 0, 0)),
                      pl.BlockSpec((None, NSTREAM, HEAD_DIM), lambda b: (b, 0, 0)),
                      pl.BlockSpec((None, NSTREAM, 1), lambda b: (b, 0, 0))]
        out_shape += [jax.ShapeDtypeStruct((nb, depth, 2, N_HEADS, HEAD_DIM, HEAD_DIM), F32),
                      jax.ShapeDtypeStruct((nb, NSTREAM, HEAD_DIM), F32),
                      jax.ShapeDtypeStruct((nb, NSTREAM, 1), F32)]
        if prev_c is not None:
            aliases[len(pre_args)] = 1
            pre_args.append(prev_c)
            pre_specs.append(pl.BlockSpec(memory_space=pl.ANY))
    body = functools.partial(_mlstm_kernel, zero_init=init_state is None, emit_state=state_out is not None)
    return pl.pallas_call(
        _skip_refs(body, len(pre_args)),
        grid=(nb,),
        in_specs=pre_specs + in_specs,
        out_specs=out_specs,
        out_shape=out_shape,
        input_output_aliases=aliases,
        scratch_shapes=[pltpu.VMEM((nsteps, NSTREAM, t), F32),
                        pltpu.VMEM((nsteps, NSTREAM, HEAD_DIM), F32),
                        pltpu.VMEM((nsteps, t, LANES), F32),
                        pltpu.VMEM((2, seq, WC), F32)]
        + [pltpu.VMEM((HEAD_DIM, HEAD_DIM), F32)] * NSTREAM + [pltpu.VMEM((1, HEAD_DIM), F32)] * NSTREAM,
        compiler_params=_params("parallel"),
        name=f"mlstm_seq{seq}",
    )(*pre_args, *args)


def _merge_kernel(h_ref, a_ref, p_ref, c_ref, wga_ref, wgb_ref, wgc_ref, bga_ref, bgb_ref, bgc_ref,
                  wa_ref, wp_ref, ps_ref, wc_ref, o_ref):
    h = h_ref[...]

    def gate(w_ref, b_ref):
        return jax.nn.sigmoid(jnp.dot(h, w_ref[...], preferred_element_type=F32) + b_ref[...])

    ya = jnp.dot(a_ref[...], wa_ref[...], preferred_element_type=F32)
    yb = jnp.dot(p_ref[...], wp_ref[...], preferred_element_type=F32) * ps_ref[...]
    yc = jnp.dot(c_ref[...], wc_ref[...], preferred_element_type=F32)
    mix = gate(wga_ref, bga_ref) * ya + gate(wgb_ref, bgb_ref) * yb + gate(wgc_ref, bgc_ref) * yc
    o_ref[...] = mix.astype(o_ref.dtype)


def _merge(h, a_act, p, hc, w_all, b_all, layer, col0, w_a_out, w_pool, pool_scale, w_c_out):
    m, d = h.shape
    tm = 1024
    tn = d // N_POOL_GROUPS
    nj = d // tn
    c0 = col0 // tn

    def wg(k):
        return pl.BlockSpec((None, d, tn), lambda i, j: (layer, 0, c0 + k * nj + j))

    def bg(k):
        return pl.BlockSpec((None, 1, tn), lambda i, j: (layer, 0, c0 + k * nj + j))

    return pl.pallas_call(
        _merge_kernel,
        grid=(m // tm, nj),
        in_specs=[pl.BlockSpec((tm, d), lambda i, j: (i, 0)),
                  pl.BlockSpec((tm, WA), lambda i, j: (i, 0)),
                  pl.BlockSpec((tm, POOL_GC), lambda i, j: (i, j)),
                  pl.BlockSpec((tm, WC), lambda i, j: (i, 0)),
                  wg(0), wg(1), wg(2), bg(0), bg(1), bg(2),
                  pl.BlockSpec((WA, tn), lambda i, j: (0, j)),
                  pl.BlockSpec((None, POOL_GC, tn), lambda i, j: (j, 0, 0)),
                  pl.BlockSpec((1, tn), lambda i, j: (0, j)),
                  pl.BlockSpec((WC, tn), lambda i, j: (0, j))],
        out_specs=pl.BlockSpec((tm, tn), lambda i, j: (i, j)),
        out_shape=jax.ShapeDtypeStruct((m, d), BF16),
        compiler_params=_params("parallel", "arbitrary"),
        name="merge",
    )(h, a_act, p, hc, w_all, w_all, w_all, b_all, b_all, b_all,
      w_a_out, w_pool, pool_scale.reshape(1, d), w_c_out)


def _proj_residual_kernel(a_ref, w_ref, x_ref, wpost_ref, gate_ref, wpre_ref, sc_ref, sh_ref, xo_ref, ho_ref,
                          *, sub):
    for r in range(a_ref.shape[0] // sub):
        rows = pl.ds(r * sub, sub)
        y = jnp.dot(a_ref[rows, :], w_ref[...], preferred_element_type=F32)
        xn = x_ref[rows, :] + gate_ref[...] * (_rms(y) * wpost_ref[...])
        xo_ref[rows, :] = xn
        hn = _rms(xn) * wpre_ref[...]
        ho_ref[rows, :] = (hn * (1.0 + sc_ref[...]) + sh_ref[...]).astype(ho_ref.dtype)


def _proj_residual(lay, a, w, x, w_post, w_pre_next, mod, gate, scale, shift, tm, sub, name):
    d = lay.d
    kdim = a.shape[1]
    row = pl.BlockSpec((1, d), lambda i: (0, 0))
    return pl.pallas_call(
        functools.partial(_proj_residual_kernel, sub=sub),
        grid=(lay.m // tm,),
        in_specs=[pl.BlockSpec((tm, kdim), lambda i: (i, 0)),
                  pl.BlockSpec((kdim, d), lambda i: (0, 0), pipeline_mode=pl.Buffered(1)),
                  pl.BlockSpec((tm, d), lambda i: (i, 0)),
                  row, lay.mod_spec(*gate, tm),
                  row, lay.mod_spec(*scale, tm), lay.mod_spec(*shift, tm)],
        out_specs=[pl.BlockSpec((tm, d), lambda i: (i, 0)), pl.BlockSpec((tm, d), lambda i: (i, 0))],
        out_shape=[jax.ShapeDtypeStruct((lay.m, d), F32), jax.ShapeDtypeStruct((lay.m, d), BF16)],
        compiler_params=_params("parallel"),
        name=name,
    )(a, w, x, w_post.reshape(1, d), mod, w_pre_next.reshape(1, d), mod, mod)


def _gelu_tanh(x):
    return 0.5 * x * (1.0 + jnp.tanh(0.7978845608028654 * (x + 0.044715 * (x * x * x))))


MXU_WIDTH = 256


def _ffn_up_kernel(h_ref, wu_ref, wg_ref, cw_ref, cb_ref, o_ref, *, shift, seg):
    h = h_ref[...]
    tm = h.shape[0]
    for c in range(o_ref.shape[1] // MXU_WIDTH):
        cols = pl.ds(c * MXU_WIDTH, MXU_WIDTH)
        u = jnp.dot(h, wu_ref[:, cols], preferred_element_type=F32)
        g = jnp.dot(h, wg_ref[:, cols], preferred_element_type=F32)
        if shift % SUBLANES == 0 and seg == tm:
            zeros = jnp.zeros((shift, MXU_WIDTH), F32)
            prev = jnp.concatenate([zeros, g[:tm - shift]], axis=0)
            nxt = jnp.concatenate([g[shift:], zeros], axis=0)
        else:
            r = lax.broadcasted_iota(jnp.int32, g.shape, 0) & (seg - 1)
            prev = jnp.where(r < shift, 0.0, pltpu.roll(g, shift, 0))
            nxt = jnp.where(r >= seg - shift, 0.0, pltpu.roll(g, tm - shift, 0))
        gc = cw_ref[0:1, cols] * prev + cw_ref[1:2, cols] * g + cw_ref[2:3, cols] * nxt + cb_ref[:, cols]
        o_ref[:, cols] = (_gelu_tanh(gc) * u).astype(o_ref.dtype)


def _ffn_up(h2, row0, rows, tm, shift, seg, w_up, layer, conv_w, conv_b, prev):
    m, d = h2.shape
    dff = w_up.shape[2] // 2
    tn = 512
    nj = dff // tn
    blk0 = row0 // tm
    pre_args, pre_specs, aliases = _into(prev)
    return pl.pallas_call(
        _skip_refs(functools.partial(_ffn_up_kernel, shift=shift, seg=seg), len(pre_args)),
        grid=(rows // tm, nj),
        in_specs=pre_specs + [pl.BlockSpec((tm, d), lambda i, j: (blk0 + i, 0)),
                              pl.BlockSpec((None, d, tn), lambda i, j: (layer, 0, j)),
                              pl.BlockSpec((None, d, tn), lambda i, j: (layer, 0, nj + j)),
                              pl.BlockSpec((3, tn), lambda i, j: (0, j)),
                              pl.BlockSpec((1, tn), lambda i, j: (0, j))],
        out_specs=pl.BlockSpec((tm, tn), lambda i, j: (blk0 + i, j)),
        out_shape=jax.ShapeDtypeStruct((m, dff), BF16),
        input_output_aliases=aliases,
        compiler_params=_params("parallel", "arbitrary"),
        name=f"ffn_up_shift{shift}",
    )(*pre_args, h2, w_up, w_up, conv_w, conv_b.reshape(1, dff))


def kernel(x_prompt, x_sample, state_C, state_n, state_m, c, c_ctx, w_ada, b_ada, norm_mix_pre, norm_mix_post, norm_ffn_pre, norm_ffn_post, w_in, b_in, conv_a_w, conv_a_b, ln_a_w, ln_a_b, w_a_out, w_pool, pool_scale, mlstm_norm_w, w_c_out, w_out, w_ffn_up, ffn_conv_w, ffn_conv_b, w_ffn_down):
    bp, lp, d = x_prompt.shape
    bs, ls, _ = x_sample.shape
    depth = w_ada.shape[0]
    lay = _Layout(bp, lp, bs, ls, d)
    assert bs < N_COND_ROWS and lay.mp % (2 * ls) == 0 and lay.ms % (2 * ls) == 0 and ls % lp == 0
    assert lp & (lp - 1) == 0 and GRID_W & (GRID_W - 1) == 0
    assert lp % SCAN_CHUNK == 0 and ls % SCAN_CHUNK == 0 and N_COLVEC * NSTREAM <= LANES

    cond = jnp.zeros((N_COND_ROWS, d), F32).at[:bs].set(c).at[bs].set(c_ctx)
    mod = _ada(cond, w_ada, b_ada).reshape(depth * N_COND_ROWS * N_MOD, 1, d)

    def regroup(a):
        gate_pad = [(0, 0)] * (a.ndim - 1) + [(0, LANES - 4 * N_HEADS)]
        return jnp.concatenate([a[..., OFF_QKV:OFF_GATES], a[..., :OFF_QKV], a[..., OFF_MERGE:],
                                jnp.pad(a[..., OFF_GATES:OFF_MERGE], gate_pad)], axis=-1)

    col_merge = Z_WIDTH
    col_gate = col_merge + N_BRANCH * d
    w_all = regroup(w_in).astype(BF16)
    b_all = regroup(b_in)[:, None, :]
    w_a_out_b = w_a_out.astype(BF16)
    w_pool_b = w_pool.astype(BF16)
    w_c_out_b = w_c_out.astype(BF16)
    w_out_b = w_out.astype(BF16)
    w_up_b = w_ffn_up.astype(BF16)
    w_down_b = w_ffn_down.astype(BF16)

    x = jnp.concatenate([x_prompt.reshape(lay.mp, d), x_sample.reshape(lay.ms, d)], axis=0)
    h = _norm_mod(lay, x, norm_mix_pre[0], mod, 0, 1, 0)
    new_c = None
    new_n, new_m = [], []
    for l in range(depth):
        z = _matmul_bias(h, w_all, b_all, l, 0, Z_WIDTH, 2048, 512, "in_proj")
        zg = _matmul_bias(h, w_all, b_all, l, col_gate, LANES, 1024, LANES, "gate_proj")[:, :4 * N_HEADS]

        conv_args = (conv_a_w[l], conv_a_b[l], ln_a_w[l], ln_a_b[l])
        a_act = _conv_a(z, 0, lay.mp, lp, *conv_args, None)
        a_act = _conv_a(z, lay.mp, lay.ms, GRID_W, *conv_args, a_act)
        p = _pool(z, 0, lay.mp, lp, 1, None)
        p = _pool(z, lay.mp, lay.ms, ls, GRID_W, p)
        hc, new_c, n_l, m_l = _mlstm(z, zg[:lay.mp], mlstm_norm_w[l], 0, bp, lp, None, l, None, (depth, new_c))
        (hc,) = _mlstm(z, zg[lay.mp:], mlstm_norm_w[l], lay.mp, bs, ls, (state_C, state_n, state_m), l, hc, None)
        new_n.append(n_l.reshape(bp, 2, N_HEADS, HEAD_DIM))
        new_m.append(m_l.reshape(bp, 2, N_HEADS))

        mix_in = _merge(h, a_act, p, hc, w_all, b_all, l, col_merge,
                        w_a_out_b[l], w_pool_b[l], pool_scale[l], w_c_out_b[l])
        x, h2 = _proj_residual(lay, mix_in, w_out_b[l], x, norm_mix_post[l], norm_ffn_pre[l], mod,
                               (l, 2), (l, 4), (l, 3), 512, 256, "out_proj")
        ffn_args = (w_up_b, l, ffn_conv_w[l], ffn_conv_b[l])
        tm_ffn = ls
        f_in = _ffn_up(h2, 0, lay.mp, tm_ffn, 1, lp, *ffn_args, None)
        f_in = _ffn_up(h2, lay.mp, lay.ms, tm_ffn, GRID_W, ls, *ffn_args, f_in)
        nl = min(l + 1, depth - 1)
        x, h = _proj_residual(lay, f_in, w_down_b[l], x, norm_ffn_post[l], norm_mix_pre[nl], mod,
                              (l, 5), (nl, 1), (nl, 0), 256, 128, "ffn_down")

    y_prompt = x[:lay.mp].reshape(bp, lp, d)
    y_sample = x[lay.mp:].reshape(bs, ls, d)
    return (y_prompt, y_sample, new_c, jnp.stack(new_n, axis=1), jnp.stack(new_m, axis=1))
```

```python
import functools

import jax
import jax.numpy as jnp
from jax import lax
from jax.experimental import pallas as pl
from jax.experimental.pallas import tpu as pltpu

GRID_W = 64
WA = 512
WB = 512
N_POOL_GROUPS = 4
POOL_GC = WB // N_POOL_GROUPS
POOL_WINDOWS = (2, 4, 8, 16)
WC = 1024
N_HEADS = 4
HEAD_DIM = WC // N_HEADS
CONV_A_WIDTH = 31
CHUNK = 64
N_BRANCH = 3
N_MOD = 6
EPS = 1e-6

OFF_POOL = 2 * WA
OFF_QKV = OFF_POOL + WB
OFF_OG = OFF_QKV + 3 * WC
OFF_GATES = OFF_OG + WC
OFF_MERGE = OFF_GATES + 4 * N_HEADS

Z_QKV = 0
Z_OG = Z_QKV + 3 * WC
Z_GLU = Z_OG + WC
Z_POOL = Z_GLU + 2 * WA
Z_WIDTH = Z_POOL + WB

LANES = 128
SUBLANES = 8
VMEM_LIMIT_BYTES = 56 * 1024 * 1024

N_COND_ROWS = 8
BF16 = jnp.bfloat16
F32 = jnp.float32


def _params(*sem):
    return pltpu.CompilerParams(dimension_semantics=sem, vmem_limit_bytes=VMEM_LIMIT_BYTES)


def _rms(x):
    return x * lax.rsqrt(jnp.mean(x * x, axis=-1, keepdims=True) + EPS)


def _ada_kernel(c_ref, w_ref, b_ref, o_ref):
    c = c_ref[...]
    a = c * jax.nn.sigmoid(c)
    o_ref[...] = jnp.dot(a, w_ref[...], preferred_element_type=F32) + b_ref[...]


def _ada(cond, w_ada, b_ada):
    depth, d, n = w_ada.shape
    tn = 1024
    return pl.pallas_call(
        _ada_kernel,
        grid=(depth, n // tn),
        in_specs=[pl.BlockSpec((N_COND_ROWS, d), lambda l, j: (0, 0)),
                  pl.BlockSpec((None, d, tn), lambda l, j: (l, 0, j)),
                  pl.BlockSpec((None, 1, tn), lambda l, j: (l, 0, j))],
        out_specs=pl.BlockSpec((None, N_COND_ROWS, tn), lambda l, j: (l, 0, j)),
        out_shape=jax.ShapeDtypeStruct((depth, N_COND_ROWS, n), F32),
        compiler_params=_params("arbitrary", "arbitrary"),
        name="ada",
    )(cond, w_ada, b_ada.reshape(depth, 1, n))


class _Layout:
    def __init__(self, bp, lp, bs, ls, d):
        self.bp, self.lp, self.bs, self.ls, self.d = bp, lp, bs, ls, d
        self.mp = bp * lp
        self.ms = bs * ls
        self.m = self.mp + self.ms

    def mod_spec(self, layer, k, tm):
        mp, ls, bs = self.mp, self.ls, self.bs

        def index(i, *_):
            start = i * tm
            row = jnp.where(start < mp, bs, (start - mp) // ls)
            return ((layer * N_COND_ROWS + row) * N_MOD + k, 0, 0)

        return pl.BlockSpec((None, 1, self.d), index)


def _norm_mod_kernel(x_ref, w_ref, sc_ref, sh_ref, o_ref):
    h = _rms(x_ref[...]) * w_ref[...]
    o_ref[...] = (h * (1.0 + sc_ref[...]) + sh_ref[...]).astype(o_ref.dtype)


def _norm_mod(lay, x, w, mod, layer, k_scale, k_shift):
    tm = 512
    d = lay.d
    return pl.pallas_call(
        _norm_mod_kernel,
        grid=(lay.m // tm,),
        in_specs=[pl.BlockSpec((tm, d), lambda i: (i, 0)),
                  pl.BlockSpec((1, d), lambda i: (0, 0)),
                  lay.mod_spec(layer, k_scale, tm),
                  lay.mod_spec(layer, k_shift, tm)],
        out_specs=pl.BlockSpec((tm, d), lambda i: (i, 0)),
        out_shape=jax.ShapeDtypeStruct((lay.m, d), BF16),
        compiler_params=_params("parallel"),
        name="norm_mod",
    )(x, w.reshape(1, d), mod, mod)


def _matmul_bias_kernel(a_ref, w_ref, b_ref, o_ref):
    o_ref[...] = jnp.dot(a_ref[...], w_ref[...], preferred_element_type=F32) + b_ref[...]


def _matmul_bias(a, w, b, layer, col0, n, tm, tn, name):
    m, k = a.shape
    c0 = col0 // tn
    return pl.pallas_call(
        _matmul_bias_kernel,
        grid=(m // tm, n // tn),
        in_specs=[pl.BlockSpec((tm, k), lambda i, j: (i, 0)),
                  pl.BlockSpec((None, k, tn), lambda i, j: (layer, 0, c0 + j)),
                  pl.BlockSpec((None, 1, tn), lambda i, j: (layer, 0, c0 + j))],
        out_specs=pl.BlockSpec((tm, tn), lambda i, j: (i, j)),
        out_shape=jax.ShapeDtypeStruct((m, n), F32),
        compiler_params=_params("parallel", "arbitrary"),
        name=name,
    )(a, w, b)


def _cast_cols_kernel(*refs, shift):
    o_ref = refs[-1]
    tn = o_ref.shape[1]
    if shift == 0:
        o_ref[...] = refs[0][...].astype(o_ref.dtype)
    else:
        both = jnp.concatenate([refs[0][...], refs[1][...]], axis=1)
        o_ref[...] = pltpu.roll(both, 2 * tn - shift, 1)[:, :tn].astype(o_ref.dtype)


def _cast_cols(w, src_block, nblocks, shift, tn, name):
    depth, k, n_src = w.shape
    last = pl.cdiv(n_src, tn) - 1
    in_specs = [pl.BlockSpec((None, k, tn), lambda l, j: (l, 0, src_block(j)))]
    if shift:
        in_specs.append(pl.BlockSpec((None, k, tn), lambda l, j: (l, 0, jnp.minimum(src_block(j) + 1, last))))
    return pl.pallas_call(
        functools.partial(_cast_cols_kernel, shift=shift),
        grid=(depth, nblocks),
        in_specs=in_specs,
        out_specs=pl.BlockSpec((None, k, tn), lambda l, j: (l, 0, j)),
        out_shape=jax.ShapeDtypeStruct((depth, k, nblocks * tn), BF16),
        compiler_params=_params("parallel", "parallel"),
        name=name,
    )(*([w] * len(in_specs)))


def _into(prev):
    if prev is None:
        return [], [], {}
    return [prev], [pl.BlockSpec(memory_space=pl.ANY)], {0: 0}


def _skip_refs(kernel_fn, n):
    if n == 0:
        return kernel_fn
    return lambda *refs: kernel_fn(*refs[n:])


CONV_PAD = 16
CONV_ROWS = 32


def _conv_a_kernel(z_ref, w_ref, b_ref, lw_ref, lb_ref, o_ref, pad_ref, shifted_ref, *, seg):
    t = z_ref.shape[0]
    stride = seg + 2 * CONV_PAD
    half = CONV_A_WIDTH // 2
    zeros = jnp.zeros((CONV_PAD, WA), F32)
    for s in range(t // seg):
        base = s * stride
        z = z_ref[pl.ds(s * seg, seg), :]
        pad_ref[pl.ds(base, CONV_PAD), :] = zeros
        pad_ref[pl.ds(base + CONV_PAD, seg), :] = z[:, :WA] * jax.nn.sigmoid(z[:, WA:])
        pad_ref[pl.ds(base + CONV_PAD + seg, CONV_PAD), :] = zeros
    nshift = pad_ref.shape[0] - SUBLANES
    for rho in range(1, SUBLANES):
        shifted_ref[rho - 1, pl.ds(0, nshift), :] = pad_ref[pl.ds(rho, nshift), :]
    for s in range(t // seg):
        for c in range(seg // CONV_ROWS):
            row0 = s * stride + CONV_PAD + c * CONV_ROWS
            acc = jnp.broadcast_to(b_ref[...], (CONV_ROWS, WA))
            for tap in range(CONV_A_WIDTH):
                rho = (tap - half) % SUBLANES
                rows = pl.ds(row0 + tap - half - rho, CONV_ROWS)
                window = pad_ref[rows, :] if rho == 0 else shifted_ref[rho - 1, rows, :]
                acc = acc + window * w_ref[pl.ds(tap, 1), :]
            mu = jnp.mean(acc, axis=-1, keepdims=True)
            cen = acc - mu
            var = jnp.mean(cen * cen, axis=-1, keepdims=True)
            y = cen * lax.rsqrt(var + EPS) * lw_ref[...] + lb_ref[...]
            y = y * jax.nn.sigmoid(y)
            o_ref[pl.ds(s * seg + c * CONV_ROWS, CONV_ROWS), :] = y.astype(o_ref.dtype)


def _conv_a(z, row0, rows, seg, conv_w, conv_b, ln_w, ln_b, prev):
    t = 256
    nseg = t // seg
    blk0 = row0 // t
    pre_args, pre_specs, aliases = _into(prev)
    return pl.pallas_call(
        _skip_refs(functools.partial(_conv_a_kernel, seg=seg), len(pre_args)),
        grid=(rows // t,),
        in_specs=pre_specs + [pl.BlockSpec((t, 2 * WA), lambda i: (blk0 + i, Z_GLU // (2 * WA))),
                              pl.BlockSpec((CONV_A_WIDTH, WA), lambda i: (0, 0)),
                              pl.BlockSpec((1, WA), lambda i: (0, 0)),
                              pl.BlockSpec((1, WA), lambda i: (0, 0)),
                              pl.BlockSpec((1, WA), lambda i: (0, 0))],
        out_specs=pl.BlockSpec((t, WA), lambda i: (blk0 + i, 0)),
        out_shape=jax.ShapeDtypeStruct((z.shape[0], WA), BF16),
        input_output_aliases=aliases,
        scratch_shapes=[pltpu.VMEM((nseg * (seg + 2 * CONV_PAD), WA), F32),
                        pltpu.VMEM((SUBLANES - 1, nseg * (seg + 2 * CONV_PAD), WA), F32)],
        compiler_params=_params("parallel"),
        name=f"conv_a_seg{seg}",
    )(*pre_args, z, conv_w, conv_b.reshape(1, WA), ln_w.reshape(1, WA), ln_b.reshape(1, WA))


POOL_ROWS = 64
POOL_HALF_MAX = max(POOL_WINDOWS) // 2


def _pool_kernel(z_ref, o_ref, pad_ref, *, dil):
    t = z_ref.shape[0]
    npos = t // dil
    padr = POOL_HALF_MAX * dil
    zeros = jnp.zeros((padr, WB), F32)
    pad_ref[pl.ds(0, padr), :] = zeros
    pad_ref[pl.ds(padr, t), :] = z_ref[...]
    pad_ref[pl.ds(padr + t, padr), :] = zeros
    for c in range(t // POOL_ROWS):
        r0 = c * POOL_ROWS
        pos = (r0 + lax.broadcasted_iota(jnp.int32, (POOL_ROWS, POOL_GC), 0)) >> (dil.bit_length() - 1)
        for g, win in enumerate(POOL_WINDOWS):
            lanes = pl.ds(g * POOL_GC, POOL_GC)
            acc = None
            for o in range(-(win // 2), win - win // 2):
                term = pad_ref[pl.ds(padr + r0 + o * dil, POOL_ROWS), lanes]
                acc = term if acc is None else acc + term
            lo = jnp.maximum(pos - win // 2, 0)
            hi = jnp.minimum(pos - win // 2 + win, npos)
            cnt = (hi - lo).astype(F32)
            x = pad_ref[pl.ds(padr + r0, POOL_ROWS), lanes]
            o_ref[pl.ds(r0, POOL_ROWS), lanes] = (acc / cnt - x).astype(o_ref.dtype)


def _pool(z, row0, rows, t, dil, prev):
    blk0 = row0 // t
    pre_args, pre_specs, aliases = _into(prev)
    return pl.pallas_call(
        _skip_refs(functools.partial(_pool_kernel, dil=dil), len(pre_args)),
        grid=(rows // t,),
        in_specs=pre_specs + [pl.BlockSpec((t, WB), lambda i: (blk0 + i, Z_POOL // WB))],
        out_specs=pl.BlockSpec((t, WB), lambda i: (blk0 + i, 0)),
        out_shape=jax.ShapeDtypeStruct((z.shape[0], WB), BF16),
        input_output_aliases=aliases,
        scratch_shapes=[pltpu.VMEM((t + 2 * POOL_HALF_MAX * dil, WB), F32)],
        compiler_params=_params("parallel"),
        name=f"pool_dil{dil}",
    )(*pre_args, z)


SCAN_CHUNK = 256
NSTREAM = 2 * N_HEADS
N_COLVEC = 4


def _log_sigmoid(x):
    return jnp.minimum(x, 0.0) - jnp.log1p(jnp.exp(-jnp.abs(x)))


def _lane_scan(x, op, ident, reverse):
    lane = lax.broadcasted_iota(jnp.int32, x.shape, 1)
    n = x.shape[1]
    sh = 1
    while sh < n:
        if reverse:
            x = op(x, jnp.where(lane < n - sh, pltpu.roll(x, n - sh, 1), ident))
        else:
            x = op(x, jnp.where(lane >= sh, pltpu.roll(x, sh, 1), ident))
        sh *= 2
    return x


def _mlstm_kernel(*refs, zero_init, emit_state):
    q_ref, k_ref, v_ref, og_ref, g_ref, nw_ref = refs[:6]
    pos = 6
    if not zero_init:
        c0_ref, n0_ref, m0_ref = refs[pos:pos + 3]
        pos += 3
    o_ref = refs[pos]
    pos += 1
    if emit_state:
        co_ref, no_ref, mo_ref = refs[pos:pos + 3]
        pos += 3
    rowbuf, decbuf, colbuf, h_scr = refs[pos:pos + 4]
    c_scr = refs[pos + 4:pos + 4 + NSTREAM]
    n_scr = refs[pos + 4 + NSTREAM:pos + 4 + 2 * NSTREAM]

    t = SCAN_CHUNK
    nsteps = q_ref.shape[0] // t

    log_i = g_ref[0]
    log_f = _log_sigmoid(g_ref[1])
    backward = (lax.broadcasted_iota(jnp.int32, log_i.shape, 0) & N_HEADS) != 0

    def scan(x, op, ident):
        return jnp.where(backward, _lane_scan(x, op, ident, True), _lane_scan(x, op, ident, False))

    b = scan(log_f, jnp.add, 0.0)
    c = log_i - b
    c_run = scan(c, jnp.maximum, -jnp.inf)
    b_tot = jnp.sum(log_f, axis=1, keepdims=True)
    c_tot = jnp.max(c, axis=1, keepdims=True)
    m = jnp.zeros((NSTREAM, 1), F32) if zero_init else m0_ref[...]
    unused_rows = jnp.zeros((LANES - N_COLVEC * NSTREAM, t), F32)
    for i in range(nsteps):
        sl = slice(i * NSTREAM, (i + 1) * NSTREAM)
        m_last = jnp.maximum(m, c_tot[sl])
        m_run = jnp.maximum(m, c_run[sl])
        packed = [m_run, jnp.exp(m - m_run), jnp.exp(-(b[sl] + m_run)), jnp.exp(c[sl] - m_last), unused_rows]
        colbuf[i] = jnp.concatenate(packed, axis=0).T
        rowbuf[i] = c[sl]
        decbuf[i] = jnp.broadcast_to(jnp.exp(m - m_last), (NSTREAM, HEAD_DIM))
        m = b_tot[sl] + m_last
    if emit_state:
        mo_ref[...] = m

    if not zero_init:
        for s in range(NSTREAM):
            d, hd = divmod(s, N_HEADS)
            c_scr[s][...] = c0_ref[d, hd]
            n_scr[s][...] = n0_ref[s:s + 1, :]

    t_idx = lax.broadcasted_iota(jnp.int32, (t, t), 0)
    s_idx = lax.broadcasted_iota(jnp.int32, (t, t), 1)
    scanned = (s_idx <= t_idx, s_idx >= t_idx)

    def step(i, has_state, update_state):
        col = colbuf[i]

        def colvec(vec, s):
            return col[:, vec * NSTREAM + s:vec * NSTREAM + s + 1]

        def where(s):
            d, hd = divmod(s, N_HEADS)
            cidx = i if d == 0 else nsteps - 1 - i
            return d, pl.ds(pl.multiple_of(cidx * t, t), t), pl.ds(hd * HEAD_DIM, HEAD_DIM)

        scores = []
        for s in range(NSTREAM):
            d, rows, cols = where(s)
            dmat = jnp.exp(jnp.where(scanned[d], rowbuf[i, s:s + 1, :] - colvec(0, s), -jnp.inf))
            q = q_ref[rows, cols] * (HEAD_DIM ** -0.5)
            scores.append(lax.dot_general(q, k_ref[rows, cols], (((1,), (1,)), ((), ())),
                                          preferred_element_type=F32) * dmat)
        for s in range(NSTREAM):
            d, rows, cols = where(s)
            num = jnp.dot(scores[s], v_ref[rows, cols], preferred_element_type=F32)
            den = jnp.sum(scores[s], axis=1, keepdims=True)
            if has_state:
                q = q_ref[rows, cols] * (HEAD_DIM ** -0.5)
                w_int = colvec(1, s)
                num = num + w_int * jnp.dot(q, c_scr[s][...], preferred_element_type=F32)
                den = den + w_int * jnp.sum(q * n_scr[s][...], axis=1, keepdims=True)
            h_scr[d, rows, cols] = num * (1.0 / jnp.maximum(jnp.abs(den), colvec(2, s)))
        if update_state:
            for s in range(NSTREAM):
                d, rows, cols = where(s)
                wk = k_ref[rows, cols] * colvec(3, s)
                c_new = lax.dot_general(wk, v_ref[rows, cols], (((0,), (0,)), ((), ())),
                                        preferred_element_type=F32)
                n_new = jnp.sum(wk, axis=0, keepdims=True)
                if has_state:
                    dec = decbuf[i, s:s + 1, :]
                    c_new = c_new + dec * c_scr[s][...]
                    n_new = n_new + dec * n_scr[s][...]
                c_scr[s][...] = c_new
                n_scr[s][...] = n_new

    first = 1 if zero_init else 0
    last = nsteps if emit_state else nsteps - 1
    if zero_init:
        step(0, False, emit_state or nsteps > 1)
    if last - first == 1:
        step(first, True, True)
    elif last > first:
        def body(i, carry):
            step(i, True, True)
            return carry
        lax.fori_loop(first, last, body, 0)
    if not emit_state and not (zero_init and nsteps == 1):
        step(nsteps - 1, True, False)

    for hd in range(N_HEADS):
        cols = pl.ds(hd * HEAD_DIM, HEAD_DIM)
        hn = _rms(h_scr[0, :, cols] + h_scr[1, :, cols])
        o_ref[:, cols] = (hn * nw_ref[:, cols] * jax.nn.sigmoid(og_ref[:, cols])).astype(o_ref.dtype)
    if emit_state:
        for s in range(NSTREAM):
            d, hd = divmod(s, N_HEADS)
            co_ref[d, hd] = c_scr[s][...]
            no_ref[s:s + 1, :] = n_scr[s][...]


def _mlstm(z, gates, norm_w, row0, nb, seq, init_state, layer, prev_hc, state_out):
    m_tot = z.shape[0]
    t = SCAN_CHUNK
    nsteps = seq // t
    blk0 = row0 // seq
    nrow = nsteps * NSTREAM
    g = gates.reshape(nb, nsteps, t, 2, 2, N_HEADS).transpose(0, 4, 3, 1, 5, 2)
    g = jnp.concatenate([g[:, :, 0], g[:, :, 1, ::-1]], axis=3)
    g = g.reshape(nb, 2, nrow, t)

    big = seq * WC * 4 > (2 << 20)

    def zspec(off):
        idx = lambda b: (blk0 + b, off // WC)
        return pl.BlockSpec((seq, WC), idx, pipeline_mode=pl.Buffered(1)) if big else pl.BlockSpec((seq, WC), idx)

    in_specs = [zspec(Z_QKV), zspec(Z_QKV + WC), zspec(Z_QKV + 2 * WC), zspec(Z_OG),
                pl.BlockSpec((None, 2, nrow, t), lambda b: (b, 0, 0, 0)),
                pl.BlockSpec((1, WC), lambda b: (0, 0))]
    args = [z, z, z, z, g, norm_w.reshape(1, WC)]
    if init_state is not None:
        sc, sn, sm = init_state
        depth = sc.shape[1]
        in_specs += [pl.BlockSpec((None, None, 2, N_HEADS, HEAD_DIM, HEAD_DIM), lambda b: (b, layer, 0, 0, 0, 0)),
                     pl.BlockSpec((None, None, NSTREAM, HEAD_DIM), lambda b: (b, layer, 0, 0)),
                     pl.BlockSpec((None, None, NSTREAM, 1), lambda b: (b, layer, 0, 0))]
        args += [sc, sn.reshape(nb, depth, NSTREAM, HEAD_DIM), sm.reshape(nb, depth, NSTREAM, 1)]
    out_specs = [pl.BlockSpec((seq, WC), lambda b: (blk0 + b, 0))]
    out_shape = [jax.ShapeDtypeStruct((m_tot, WC), BF16)]
    aliases = {}
    pre_args, pre_specs = [], []
    if prev_hc is not None:
        pre_args.append(prev_hc)
        pre_specs.append(pl.BlockSpec(memory_space=pl.ANY))
        aliases[0] = 0
    if state_out is not None:
        depth, prev_c = state_out
        out_specs += [pl.BlockSpec((None, None, 2, N_HEADS, HEAD_DIM, HEAD_DIM), lambda b: (b, layer, 0, 0, 0, 0)),
                      pl.BlockSpec((None, NSTREAM, HEAD_DIM), lambda b: (b, 0, 0)),
                      pl.BlockSpec((None, NSTREAM, 1), lambda b: (b, 0, 0))]
        out_shape += [jax.ShapeDtypeStruct((nb, depth, 2, N_HEADS, HEAD_DIM, HEAD_DIM), F32),
                      jax.ShapeDtypeStruct((nb, NSTREAM, HEAD_DIM), F32),
                      jax.ShapeDtypeStruct((nb, NSTREAM, 1), F32)]
        if prev_c is not None:
            aliases[len(pre_args)] = 1
            pre_args.append(prev_c)
            pre_specs.append(pl.BlockSpec(memory_space=pl.ANY))
    body = functools.partial(_mlstm_kernel, zero_init=init_state is None, emit_state=state_out is not None)
    return pl.pallas_call(
        _skip_refs(body, len(pre_args)),
        grid=(nb,),
        in_specs=pre_specs + in_specs,
        out_specs=out_specs,
        out_shape=out_shape,
        input_output_aliases=aliases,
        scratch_shapes=[pltpu.VMEM((nsteps, NSTREAM, t), F32),
                        pltpu.VMEM((nsteps, NSTREAM, HEAD_DIM), F32),
                        pltpu.VMEM((nsteps, t, LANES), F32),
                        pltpu.VMEM((2, seq, WC), F32)]
        + [pltpu.VMEM((HEAD_DIM, HEAD_DIM), F32)] * NSTREAM + [pltpu.VMEM((1, HEAD_DIM), F32)] * NSTREAM,
        compiler_params=_params("parallel"),
        name=f"mlstm_seq{seq}",
    )(*pre_args, *args)


def _merge_kernel(h_ref, a_ref, p_ref, c_ref, wga_ref, wgb_ref, wgc_ref, bga_ref, bgb_ref, bgc_ref,
                  wa_ref, wp_ref, ps_ref, wc_ref, o_ref):
    h = h_ref[...]

    def gate(w_ref, b_ref):
        return jax.nn.sigmoid(jnp.dot(h, w_ref[...], preferred_element_type=F32) + b_ref[...])

    ya = jnp.dot(a_ref[...], wa_ref[...], preferred_element_type=F32)
    yb = jnp.dot(p_ref[...], wp_ref[...], preferred_element_type=F32) * ps_ref[...]
    yc = jnp.dot(c_ref[...], wc_ref[...], preferred_element_type=F32)
    mix = gate(wga_ref, bga_ref) * ya + gate(wgb_ref, bgb_ref) * yb + gate(wgc_ref, bgc_ref) * yc
    o_ref[...] = mix.astype(o_ref.dtype)


def _merge(h, a_act, p, hc, w_all, b_all, layer, col0, w_a_out, w_pool, pool_scale, w_c_out):
    m, d = h.shape
    tm = 1024
    tn = d // N_POOL_GROUPS
    nj = d // tn
    c0 = col0 // tn

    def wg(k):
        return pl.BlockSpec((None, d, tn), lambda i, j: (layer, 0, c0 + k * nj + j))

    def bg(k):
        return pl.BlockSpec((None, 1, tn), lambda i, j: (layer, 0, c0 + k * nj + j))

    return pl.pallas_call(
        _merge_kernel,
        grid=(m // tm, nj),
        in_specs=[pl.BlockSpec((tm, d), lambda i, j: (i, 0)),
                  pl.BlockSpec((tm, WA), lambda i, j: (i, 0)),
                  pl.BlockSpec((tm, POOL_GC), lambda i, j: (i, j)),
                  pl.BlockSpec((tm, WC), lambda i, j: (i, 0)),
                  wg(0), wg(1), wg(2), bg(0), bg(1), bg(2),
                  pl.BlockSpec((WA, tn), lambda i, j: (0, j)),
                  pl.BlockSpec((None, POOL_GC, tn), lambda i, j: (j, 0, 0)),
                  pl.BlockSpec((1, tn), lambda i, j: (0, j)),
                  pl.BlockSpec((WC, tn), lambda i, j: (0, j))],
        out_specs=pl.BlockSpec((tm, tn), lambda i, j: (i, j)),
        out_shape=jax.ShapeDtypeStruct((m, d), BF16),
        compiler_params=_params("parallel", "arbitrary"),
        name="merge",
    )(h, a_act, p, hc, w_all, w_all, w_all, b_all, b_all, b_all,
      w_a_out, w_pool, pool_scale.reshape(1, d), w_c_out)


def _proj_residual_kernel(a_ref, w_ref, x_ref, wpost_ref, gate_ref, wpre_ref, sc_ref, sh_ref, xo_ref, ho_ref,
                          *, sub):
    for r in range(a_ref.shape[0] // sub):
        rows = pl.ds(r * sub, sub)
        y = jnp.dot(a_ref[rows, :], w_ref[...], preferred_element_type=F32)
        xn = x_ref[rows, :] + gate_ref[...] * (_rms(y) * wpost_ref[...])
        xo_ref[rows, :] = xn
        if ho_ref is not None:
            hn = _rms(xn) * wpre_ref[...]
            ho_ref[rows, :] = (hn * (1.0 + sc_ref[...]) + sh_ref[...]).astype(ho_ref.dtype)


def _proj_residual_last_kernel(a_ref, w_ref, x_ref, wpost_ref, gate_ref, xo_ref, *, sub):
    _proj_residual_kernel(a_ref, w_ref, x_ref, wpost_ref, gate_ref, None, None, None, xo_ref, None, sub=sub)


def _proj_residual(lay, a, w, x, w_post, mod, gate, tm, sub, name, nxt=None, row0=0, rows=None):
    d = lay.d
    kdim = a.shape[1]
    rows = lay.m if rows is None else rows
    blk0 = row0 // tm
    row = pl.BlockSpec((1, d), lambda i: (0, 0))
    tile = lambda width: pl.BlockSpec((tm, width), lambda i: (blk0 + i, 0))

    def mod_at(layer, k):
        spec = lay.mod_spec(layer, k, tm)
        return pl.BlockSpec(spec.block_shape, lambda i: spec.index_map(blk0 + i))

    in_specs = [tile(kdim), pl.BlockSpec((kdim, d), lambda i: (0, 0), pipeline_mode=pl.Buffered(1)), tile(d),
                row, mod_at(*gate)]
    args = [a, w, x, w_post.reshape(1, d), mod]
    out_tile = pl.BlockSpec((tm, d), lambda i: (i, 0))
    if nxt is None:
        body = functools.partial(_proj_residual_last_kernel, sub=sub)
        out_specs, out_shape = out_tile, jax.ShapeDtypeStruct((rows, d), F32)
    else:
        assert row0 == 0 and rows == lay.m
        w_pre_next, scale, shift = nxt
        body = functools.partial(_proj_residual_kernel, sub=sub)
        in_specs += [row, mod_at(*scale), mod_at(*shift)]
        args += [w_pre_next.reshape(1, d), mod, mod]
        out_specs = [out_tile, out_tile]
        out_shape = [jax.ShapeDtypeStruct((rows, d), F32), jax.ShapeDtypeStruct((rows, d), BF16)]
    return pl.pallas_call(
        body,
        grid=(rows // tm,),
        in_specs=in_specs,
        out_specs=out_specs,
        out_shape=out_shape,
        compiler_params=_params("parallel"),
        name=name,
    )(*args)


def _gelu_tanh(x):
    return 0.5 * x * (1.0 + jnp.tanh(0.7978845608028654 * (x + 0.044715 * (x * x * x))))


MXU_WIDTH = 256


def _ffn_up_kernel(h_ref, wu_ref, wg_ref, cw_ref, cb_ref, o_ref, *, shift, seg):
    h = h_ref[...]
    tm = h.shape[0]
    for c in range(o_ref.shape[1] // MXU_WIDTH):
        cols = pl.ds(c * MXU_WIDTH, MXU_WIDTH)
        u = jnp.dot(h, wu_ref[:, cols], preferred_element_type=F32)
        g = jnp.dot(h, wg_ref[:, cols], preferred_element_type=F32)
        if shift % SUBLANES == 0 and seg == tm:
            zeros = jnp.zeros((shift, MXU_WIDTH), F32)
            prev = jnp.concatenate([zeros, g[:tm - shift]], axis=0)
            nxt = jnp.concatenate([g[shift:], zeros], axis=0)
        else:
            r = lax.broadcasted_iota(jnp.int32, g.shape, 0) & (seg - 1)
            prev = jnp.where(r < shift, 0.0, pltpu.roll(g, shift, 0))
            nxt = jnp.where(r >= seg - shift, 0.0, pltpu.roll(g, tm - shift, 0))
        gc = cw_ref[0:1, cols] * prev + cw_ref[1:2, cols] * g + cw_ref[2:3, cols] * nxt + cb_ref[:, cols]
        o_ref[:, cols] = (_gelu_tanh(gc) * u).astype(o_ref.dtype)


def _ffn_up(h2, row0, rows, tm, shift, seg, w_up, layer, conv_w, conv_b, prev):
    m, d = h2.shape
    dff = w_up.shape[2] // 2
    nj = 2
    tn = dff // nj
    blk0 = row0 // tm
    pre_args, pre_specs, aliases = _into(prev)
    resident = dict(pipeline_mode=pl.Buffered(1))
    return pl.pallas_call(
        _skip_refs(functools.partial(_ffn_up_kernel, shift=shift, seg=seg), len(pre_args)),
        grid=(nj, rows // tm),
        in_specs=pre_specs + [pl.BlockSpec((tm, d), lambda j, i: (blk0 + i, 0)),
                              pl.BlockSpec((None, d, tn), lambda j, i: (layer, 0, j), **resident),
                              pl.BlockSpec((None, d, tn), lambda j, i: (layer, 0, nj + j), **resident),
                              pl.BlockSpec((3, tn), lambda j, i: (0, j)),
                              pl.BlockSpec((1, tn), lambda j, i: (0, j))],
        out_specs=pl.BlockSpec((tm, tn), lambda j, i: (blk0 + i, j)),
        out_shape=jax.ShapeDtypeStruct((m, dff), BF16),
        input_output_aliases=aliases,
        compiler_params=_params("arbitrary", "arbitrary"),
        name=f"ffn_up_shift{shift}",
    )(*pre_args, h2, w_up, w_up, conv_w, conv_b.reshape(1, dff))


def kernel(x_prompt, x_sample, state_C, state_n, state_m, c, c_ctx, w_ada, b_ada, norm_mix_pre, norm_mix_post, norm_ffn_pre, norm_ffn_post, w_in, b_in, conv_a_w, conv_a_b, ln_a_w, ln_a_b, w_a_out, w_pool, pool_scale, mlstm_norm_w, w_c_out, w_out, w_ffn_up, ffn_conv_w, ffn_conv_b, w_ffn_down):
    bp, lp, d = x_prompt.shape
    bs, ls, _ = x_sample.shape
    depth = w_ada.shape[0]
    lay = _Layout(bp, lp, bs, ls, d)
    assert bs < N_COND_ROWS and lay.mp % (2 * ls) == 0 and lay.ms % (2 * ls) == 0 and ls % lp == 0
    assert lp & (lp - 1) == 0 and GRID_W & (GRID_W - 1) == 0
    assert lp % SCAN_CHUNK == 0 and ls % SCAN_CHUNK == 0 and N_COLVEC * NSTREAM <= LANES

    cond = jnp.zeros((N_COND_ROWS, d), F32).at[:bs].set(c).at[bs].set(c_ctx)
    mod = _ada(cond, w_ada, b_ada).reshape(depth * N_COND_ROWS * N_MOD, 1, d)

    tc = 512
    assert OFF_QKV % tc == 0 and OFF_GATES % tc == 0 and 0 < OFF_MERGE - OFF_GATES < tc
    nz = OFF_GATES // tc
    w_z = _cast_cols(w_in, lambda j: (j + OFF_QKV // tc) % nz, nz, 0, tc, "cast_w_z")
    w_m = _cast_cols(w_in, lambda j: nz + j, N_BRANCH * d // tc, OFF_MERGE - OFF_GATES, tc, "cast_w_merge")
    gate_pad = (0, LANES - 4 * N_HEADS)
    w_g = jnp.pad(w_in[:, :, OFF_GATES:OFF_MERGE], ((0, 0), (0, 0), gate_pad)).astype(BF16)
    b_z = jnp.concatenate([b_in[:, OFF_QKV:OFF_GATES], b_in[:, :OFF_QKV]], axis=-1)[:, None, :]
    b_m = b_in[:, None, OFF_MERGE:]
    b_g = jnp.pad(b_in[:, OFF_GATES:OFF_MERGE], ((0, 0), gate_pad))[:, None, :]
    w_a_out_b = w_a_out.astype(BF16)
    w_pool_b = w_pool.astype(BF16)
    w_c_out_b = w_c_out.astype(BF16)
    w_out_b = w_out.astype(BF16)
    w_up_b = w_ffn_up.astype(BF16)
    w_down_b = w_ffn_down.astype(BF16)

    x = jnp.concatenate([x_prompt.reshape(lay.mp, d), x_sample.reshape(lay.ms, d)], axis=0)
    h = _norm_mod(lay, x, norm_mix_pre[0], mod, 0, 1, 0)
    new_c = None
    new_n, new_m = [], []
    for l in range(depth):
        z = _matmul_bias(h, w_z, b_z, l, 0, Z_WIDTH, 2048, 512, "in_proj")
        zg = _matmul_bias(h, w_g, b_g, l, 0, LANES, 1024, LANES, "gate_proj")[:, :4 * N_HEADS]

        conv_args = (conv_a_w[l], conv_a_b[l], ln_a_w[l], ln_a_b[l])
        a_act = _conv_a(z, 0, lay.mp, lp, *conv_args, None)
        a_act = _conv_a(z, lay.mp, lay.ms, GRID_W, *conv_args, a_act)
        p = _pool(z, 0, lay.mp, lp, 1, None)
        p = _pool(z, lay.mp, lay.ms, ls, GRID_W, p)
        hc, new_c, n_l, m_l = _mlstm(z, zg[:lay.mp], mlstm_norm_w[l], 0, bp, lp, None, l, None, (depth, new_c))
        (hc,) = _mlstm(z, zg[lay.mp:], mlstm_norm_w[l], lay.mp, bs, ls, (state_C, state_n, state_m), l, hc, None)
        new_n.append(n_l.reshape(bp, 2, N_HEADS, HEAD_DIM))
        new_m.append(m_l.reshape(bp, 2, N_HEADS))

        mix_in = _merge(h, a_act, p, hc, w_m, b_m, l, 0,
                        w_a_out_b[l], w_pool_b[l], pool_scale[l], w_c_out_b[l])
        x, h2 = _proj_residual(lay, mix_in, w_out_b[l], x, norm_mix_post[l], mod, (l, 2), 512, 256, "out_proj",
                               nxt=(norm_ffn_pre[l], (l, 4), (l, 3)))
        ffn_args = (w_up_b, l, ffn_conv_w[l], ffn_conv_b[l])
        tm_ffn = ls
        f_in = _ffn_up(h2, 0, lay.mp, tm_ffn, 1, lp, *ffn_args, None)
        f_in = _ffn_up(h2, lay.mp, lay.ms, tm_ffn, GRID_W, ls, *ffn_args, f_in)
        down_args = (lay, f_in, w_down_b[l], x, norm_ffn_post[l], mod, (l, 5), 256, 128, "ffn_down")
        if l + 1 < depth:
            x, h = _proj_residual(*down_args, nxt=(norm_mix_pre[l + 1], (l + 1, 1), (l + 1, 0)))
        else:
            y_prompt = _proj_residual(*down_args, row0=0, rows=lay.mp).reshape(bp, lp, d)
            y_sample = _proj_residual(*down_args, row0=lay.mp, rows=lay.ms).reshape(bs, ls, d)
    return (y_prompt, y_sample, new_c, jnp.stack(new_n, axis=1), jnp.stack(new_m, axis=1))
```

```python
import functools

import jax
import jax.numpy as jnp
from jax import lax
from jax.experimental import pallas as pl
from jax.experimental.pallas import tpu as pltpu

GRID_W = 64
WA = 512
WB = 512
N_POOL_GROUPS = 4
POOL_GC = WB // N_POOL_GROUPS
POOL_WINDOWS = (2, 4, 8, 16)
WC = 1024
N_HEADS = 4
HEAD_DIM = WC // N_HEADS
CONV_A_WIDTH = 31
CHUNK = 64
N_BRANCH = 3
N_MOD = 6
EPS = 1e-6

OFF_POOL = 2 * WA
OFF_QKV = OFF_POOL + WB
OFF_OG = OFF_QKV + 3 * WC
OFF_GATES = OFF_OG + WC
OFF_MERGE = OFF_GATES + 4 * N_HEADS

Z_QKV = 0
Z_OG = Z_QKV + 3 * WC
Z_GLU = Z_OG + WC
Z_POOL = Z_GLU + 2 * WA
Z_WIDTH = Z_POOL + WB

LANES = 128
SUBLANES = 8
VMEM_LIMIT_BYTES = 56 * 1024 * 1024

N_COND_ROWS = 8
BF16 = jnp.bfloat16
F32 = jnp.float32


def _params(*sem):
    return pltpu.CompilerParams(dimension_semantics=sem, vmem_limit_bytes=VMEM_LIMIT_BYTES)


def _rms(x):
    return x * lax.rsqrt(jnp.mean(x * x, axis=-1, keepdims=True) + EPS)


def _ada_kernel(c_ref, w_ref, b_ref, o_ref):
    c = c_ref[...]
    a = c * jax.nn.sigmoid(c)
    o_ref[...] = jnp.dot(a, w_ref[...], preferred_element_type=F32) + b_ref[...]


def _ada(cond, w_ada, b_ada):
    depth, d, n = w_ada.shape
    tn = 1024
    return pl.pallas_call(
        _ada_kernel,
        grid=(depth, n // tn),
        in_specs=[pl.BlockSpec((N_COND_ROWS, d), lambda l, j: (0, 0)),
                  pl.BlockSpec((None, d, tn), lambda l, j: (l, 0, j)),
                  pl.BlockSpec((None, 1, tn), lambda l, j: (l, 0, j))],
        out_specs=pl.BlockSpec((None, N_COND_ROWS, tn), lambda l, j: (l, 0, j)),
        out_shape=jax.ShapeDtypeStruct((depth, N_COND_ROWS, n), F32),
        compiler_params=_params("arbitrary", "arbitrary"),
        name="ada",
    )(cond, w_ada, b_ada.reshape(depth, 1, n))


class _Layout:
    def __init__(self, bp, lp, bs, ls, d):
        self.bp, self.lp, self.bs, self.ls, self.d = bp, lp, bs, ls, d
        self.mp = bp * lp
        self.ms = bs * ls
        self.m = self.mp + self.ms

    def mod_spec(self, layer, k, tm):
        mp, ls, bs = self.mp, self.ls, self.bs

        def index(i, *_):
            start = i * tm
            row = jnp.where(start < mp, bs, (start - mp) // ls)
            return ((layer * N_COND_ROWS + row) * N_MOD + k, 0, 0)

        return pl.BlockSpec((None, 1, self.d), index)


def _norm_mod_kernel(x_ref, w_ref, sc_ref, sh_ref, o_ref):
    h = _rms(x_ref[...]) * w_ref[...]
    o_ref[...] = (h * (1.0 + sc_ref[...]) + sh_ref[...]).astype(o_ref.dtype)


def _norm_mod(lay, x, w, mod, layer, k_scale, k_shift):
    tm = 512
    d = lay.d
    return pl.pallas_call(
        _norm_mod_kernel,
        grid=(lay.m // tm,),
        in_specs=[pl.BlockSpec((tm, d), lambda i: (i, 0)),
                  pl.BlockSpec((1, d), lambda i: (0, 0)),
                  lay.mod_spec(layer, k_scale, tm),
                  lay.mod_spec(layer, k_shift, tm)],
        out_specs=pl.BlockSpec((tm, d), lambda i: (i, 0)),
        out_shape=jax.ShapeDtypeStruct((lay.m, d), BF16),
        compiler_params=_params("parallel"),
        name="norm_mod",
    )(x, w.reshape(1, d), mod, mod)


_NT = (((1,), (1,)), ((), ()))


def _matmul_bias_kernel(a_ref, w_ref, b_ref, o_ref):
    o_ref[...] = lax.dot_general(a_ref[...], w_ref[...], _NT, preferred_element_type=F32) + b_ref[...]


def _matmul_bias(a, wt, b, layer, n, tm, tn, name):
    m, k = a.shape
    return pl.pallas_call(
        _matmul_bias_kernel,
        grid=(m // tm, n // tn),
        in_specs=[pl.BlockSpec((tm, k), lambda i, j: (i, 0)),
                  pl.BlockSpec((None, tn, k), lambda i, j: (layer, j, 0)),
                  pl.BlockSpec((None, 1, tn), lambda i, j: (layer, 0, j))],
        out_specs=pl.BlockSpec((tm, tn), lambda i, j: (i, j)),
        out_shape=jax.ShapeDtypeStruct((m, n), F32),
        compiler_params=_params("parallel", "arbitrary"),
        name=name,
    )(a, wt, b)


def _cast_rows_kernel(*refs, shift):
    o_ref = refs[-1]
    tn = o_ref.shape[0]
    if shift == 0:
        o_ref[...] = refs[0][...].astype(o_ref.dtype)
    else:
        o_ref[pl.ds(0, tn - shift), :] = refs[0][pl.ds(shift, tn - shift), :].astype(o_ref.dtype)
        o_ref[pl.ds(tn - shift, shift), :] = refs[1][pl.ds(0, shift), :].astype(o_ref.dtype)


def _cast_rows(wt, src_block, nblocks, shift, tn, name):
    depth, n_src, k = wt.shape
    last = pl.cdiv(n_src, tn) - 1
    in_specs = [pl.BlockSpec((None, tn, k), lambda l, j: (l, src_block(j), 0))]
    if shift:
        in_specs.append(pl.BlockSpec((None, tn, k), lambda l, j: (l, jnp.minimum(src_block(j) + 1, last), 0)))
    return pl.pallas_call(
        functools.partial(_cast_rows_kernel, shift=shift),
        grid=(depth, nblocks),
        in_specs=in_specs,
        out_specs=pl.BlockSpec((None, tn, k), lambda l, j: (l, j, 0)),
        out_shape=jax.ShapeDtypeStruct((depth, nblocks * tn, k), BF16),
        compiler_params=_params("parallel", "parallel"),
        name=name,
    )(*([wt] * len(in_specs)))


def _into(prev):
    if prev is None:
        return [], [], {}
    return [prev], [pl.BlockSpec(memory_space=pl.ANY)], {0: 0}


def _skip_refs(kernel_fn, n):
    if n == 0:
        return kernel_fn
    return lambda *refs: kernel_fn(*refs[n:])


CONV_PAD = 16
CONV_ROWS = 32


def _conv_a_kernel(z_ref, w_ref, b_ref, lw_ref, lb_ref, o_ref, pad_ref, shifted_ref, *, seg):
    t = z_ref.shape[0]
    stride = seg + 2 * CONV_PAD
    half = CONV_A_WIDTH // 2
    zeros = jnp.zeros((CONV_PAD, WA), F32)
    for s in range(t // seg):
        base = s * stride
        z = z_ref[pl.ds(s * seg, seg), :]
        pad_ref[pl.ds(base, CONV_PAD), :] = zeros
        pad_ref[pl.ds(base + CONV_PAD, seg), :] = z[:, :WA] * jax.nn.sigmoid(z[:, WA:])
        pad_ref[pl.ds(base + CONV_PAD + seg, CONV_PAD), :] = zeros
    nshift = pad_ref.shape[0] - SUBLANES
    for rho in range(1, SUBLANES):
        shifted_ref[rho - 1, pl.ds(0, nshift), :] = pad_ref[pl.ds(rho, nshift), :]
    for s in range(t // seg):
        for c in range(seg // CONV_ROWS):
            row0 = s * stride + CONV_PAD + c * CONV_ROWS
            acc = jnp.broadcast_to(b_ref[...], (CONV_ROWS, WA))
            for tap in range(CONV_A_WIDTH):
                rho = (tap - half) % SUBLANES
                rows = pl.ds(row0 + tap - half - rho, CONV_ROWS)
                window = pad_ref[rows, :] if rho == 0 else shifted_ref[rho - 1, rows, :]
                acc = acc + window * w_ref[pl.ds(tap, 1), :]
            mu = jnp.mean(acc, axis=-1, keepdims=True)
            cen = acc - mu
            var = jnp.mean(cen * cen, axis=-1, keepdims=True)
            y = cen * lax.rsqrt(var + EPS) * lw_ref[...] + lb_ref[...]
            y = y * jax.nn.sigmoid(y)
            o_ref[pl.ds(s * seg + c * CONV_ROWS, CONV_ROWS), :] = y.astype(o_ref.dtype)


def _conv_a(z, row0, rows, seg, conv_w, conv_b, ln_w, ln_b, prev):
    t = 256
    nseg = t // seg
    blk0 = row0 // t
    pre_args, pre_specs, aliases = _into(prev)
    return pl.pallas_call(
        _skip_refs(functools.partial(_conv_a_kernel, seg=seg), len(pre_args)),
        grid=(rows // t,),
        in_specs=pre_specs + [pl.BlockSpec((t, 2 * WA), lambda i: (blk0 + i, Z_GLU // (2 * WA))),
                              pl.BlockSpec((CONV_A_WIDTH, WA), lambda i: (0, 0)),
                              pl.BlockSpec((1, WA), lambda i: (0, 0)),
                              pl.BlockSpec((1, WA), lambda i: (0, 0)),
                              pl.BlockSpec((1, WA), lambda i: (0, 0))],
        out_specs=pl.BlockSpec((t, WA), lambda i: (blk0 + i, 0)),
        out_shape=jax.ShapeDtypeStruct((z.shape[0], WA), BF16),
        input_output_aliases=aliases,
        scratch_shapes=[pltpu.VMEM((nseg * (seg + 2 * CONV_PAD), WA), F32),
                        pltpu.VMEM((SUBLANES - 1, nseg * (seg + 2 * CONV_PAD), WA), F32)],
        compiler_params=_params("parallel"),
        name=f"conv_a_seg{seg}",
    )(*pre_args, z, conv_w, conv_b.reshape(1, WA), ln_w.reshape(1, WA), ln_b.reshape(1, WA))


POOL_ROWS = 64
POOL_HALF_MAX = max(POOL_WINDOWS) // 2


def _pool_kernel(z_ref, o_ref, pad_ref, *, dil):
    t = z_ref.shape[0]
    npos = t // dil
    padr = POOL_HALF_MAX * dil
    zeros = jnp.zeros((padr, WB), F32)
    pad_ref[pl.ds(0, padr), :] = zeros
    pad_ref[pl.ds(padr, t), :] = z_ref[...]
    pad_ref[pl.ds(padr + t, padr), :] = zeros
    for c in range(t // POOL_ROWS):
        r0 = c * POOL_ROWS
        pos = (r0 + lax.broadcasted_iota(jnp.int32, (POOL_ROWS, POOL_GC), 0)) >> (dil.bit_length() - 1)
        for g, win in enumerate(POOL_WINDOWS):
            lanes = pl.ds(g * POOL_GC, POOL_GC)
            acc = None
            for o in range(-(win // 2), win - win // 2):
                term = pad_ref[pl.ds(padr + r0 + o * dil, POOL_ROWS), lanes]
                acc = term if acc is None else acc + term
            lo = jnp.maximum(pos - win // 2, 0)
            hi = jnp.minimum(pos - win // 2 + win, npos)
            cnt = (hi - lo).astype(F32)
            x = pad_ref[pl.ds(padr + r0, POOL_ROWS), lanes]
            o_ref[pl.ds(r0, POOL_ROWS), lanes] = (acc / cnt - x).astype(o_ref.dtype)


def _pool(z, row0, rows, t, dil, prev):
    blk0 = row0 // t
    pre_args, pre_specs, aliases = _into(prev)
    return pl.pallas_call(
        _skip_refs(functools.partial(_pool_kernel, dil=dil), len(pre_args)),
        grid=(rows // t,),
        in_specs=pre_specs + [pl.BlockSpec((t, WB), lambda i: (blk0 + i, Z_POOL // WB))],
        out_specs=pl.BlockSpec((t, WB), lambda i: (blk0 + i, 0)),
        out_shape=jax.ShapeDtypeStruct((z.shape[0], WB), BF16),
        input_output_aliases=aliases,
        scratch_shapes=[pltpu.VMEM((t + 2 * POOL_HALF_MAX * dil, WB), F32)],
        compiler_params=_params("parallel"),
        name=f"pool_dil{dil}",
    )(*pre_args, z)


SCAN_CHUNK = 256
NSTREAM = 2 * N_HEADS
N_COLVEC = 4


def _log_sigmoid(x):
    return jnp.minimum(x, 0.0) - jnp.log1p(jnp.exp(-jnp.abs(x)))


def _lane_scan(x, op, ident, reverse):
    lane = lax.broadcasted_iota(jnp.int32, x.shape, 1)
    n = x.shape[1]
    sh = 1
    while sh < n:
        if reverse:
            x = op(x, jnp.where(lane < n - sh, pltpu.roll(x, n - sh, 1), ident))
        else:
            x = op(x, jnp.where(lane >= sh, pltpu.roll(x, sh, 1), ident))
        sh *= 2
    return x


def _mlstm_kernel(*refs, zero_init, emit_state):
    q_ref, k_ref, v_ref, og_ref, g_ref, nw_ref = refs[:6]
    pos = 6
    if not zero_init:
        c0_ref, n0_ref, m0_ref = refs[pos:pos + 3]
        pos += 3
    o_ref = refs[pos]
    pos += 1
    if emit_state:
        co_ref, no_ref, mo_ref = refs[pos:pos + 3]
        pos += 3
    rowbuf, decbuf, colbuf, h_scr = refs[pos:pos + 4]
    c_scr = refs[pos + 4:pos + 4 + NSTREAM]
    n_scr = refs[pos + 4 + NSTREAM:pos + 4 + 2 * NSTREAM]

    t = SCAN_CHUNK
    nsteps = q_ref.shape[0] // t

    log_i = g_ref[0]
    log_f = _log_sigmoid(g_ref[1])
    backward = (lax.broadcasted_iota(jnp.int32, log_i.shape, 0) & N_HEADS) != 0

    def scan(x, op, ident):
        return jnp.where(backward, _lane_scan(x, op, ident, True), _lane_scan(x, op, ident, False))

    b = scan(log_f, jnp.add, 0.0)
    c = log_i - b
    c_run = scan(c, jnp.maximum, -jnp.inf)
    b_tot = jnp.sum(log_f, axis=1, keepdims=True)
    c_tot = jnp.max(c, axis=1, keepdims=True)
    m = jnp.zeros((NSTREAM, 1), F32) if zero_init else m0_ref[...]
    unused_rows = jnp.zeros((LANES - N_COLVEC * NSTREAM, t), F32)
    for i in range(nsteps):
        sl = slice(i * NSTREAM, (i + 1) * NSTREAM)
        m_last = jnp.maximum(m, c_tot[sl])
        m_run = jnp.maximum(m, c_run[sl])
        packed = [m_run, jnp.exp(m - m_run), jnp.exp(-(b[sl] + m_run)), jnp.exp(c[sl] - m_last), unused_rows]
        colbuf[i] = jnp.concatenate(packed, axis=0).T
        rowbuf[i] = c[sl]
        decbuf[i] = jnp.broadcast_to(jnp.exp(m - m_last), (NSTREAM, HEAD_DIM))
        m = b_tot[sl] + m_last
    if emit_state:
        mo_ref[...] = m

    if not zero_init:
        for s in range(NSTREAM):
            d, hd = divmod(s, N_HEADS)
            c_scr[s][...] = c0_ref[d, hd]
            n_scr[s][...] = n0_ref[s:s + 1, :]

    t_idx = lax.broadcasted_iota(jnp.int32, (t, t), 0)
    s_idx = lax.broadcasted_iota(jnp.int32, (t, t), 1)
    scanned = (s_idx <= t_idx, s_idx >= t_idx)

    def step(i, has_state, update_state):
        col = colbuf[i]

        def colvec(vec, s):
            return col[:, vec * NSTREAM + s:vec * NSTREAM + s + 1]

        def where(s):
            d, hd = divmod(s, N_HEADS)
            cidx = i if d == 0 else nsteps - 1 - i
            return d, pl.ds(pl.multiple_of(cidx * t, t), t), pl.ds(hd * HEAD_DIM, HEAD_DIM)

        scores = []
        for s in range(NSTREAM):
            d, rows, cols = where(s)
            dmat = jnp.exp(jnp.where(scanned[d], rowbuf[i, s:s + 1, :] - colvec(0, s), -jnp.inf))
            q = q_ref[rows, cols] * (HEAD_DIM ** -0.5)
            scores.append(lax.dot_general(q, k_ref[rows, cols], (((1,), (1,)), ((), ())),
                                          preferred_element_type=F32) * dmat)
        for s in range(NSTREAM):
            d, rows, cols = where(s)
            num = jnp.dot(scores[s], v_ref[rows, cols], preferred_element_type=F32)
            den = jnp.sum(scores[s], axis=1, keepdims=True)
            if has_state:
                q = q_ref[rows, cols] * (HEAD_DIM ** -0.5)
                w_int = colvec(1, s)
                num = num + w_int * jnp.dot(q, c_scr[s][...], preferred_element_type=F32)
                den = den + w_int * jnp.sum(q * n_scr[s][...], axis=1, keepdims=True)
            h_scr[d, rows, cols] = num * (1.0 / jnp.maximum(jnp.abs(den), colvec(2, s)))
        if update_state:
            for s in range(NSTREAM):
                d, rows, cols = where(s)
                wk = k_ref[rows, cols] * colvec(3, s)
                c_new = lax.dot_general(wk, v_ref[rows, cols], (((0,), (0,)), ((), ())),
                                        preferred_element_type=F32)
                n_new = jnp.sum(wk, axis=0, keepdims=True)
                if has_state:
                    dec = decbuf[i, s:s + 1, :]
                    c_new = c_new + dec * c_scr[s][...]
                    n_new = n_new + dec * n_scr[s][...]
                c_scr[s][...] = c_new
                n_scr[s][...] = n_new

    first = 1 if zero_init else 0
    last = nsteps if emit_state else nsteps - 1
    if zero_init:
        step(0, False, emit_state or nsteps > 1)
    if last - first == 1:
        step(first, True, True)
    elif last > first:
        def body(i, carry):
            step(i, True, True)
            return carry
        lax.fori_loop(first, last, body, 0)
    if not emit_state and not (zero_init and nsteps == 1):
        step(nsteps - 1, True, False)

    for hd in range(N_HEADS):
        cols = pl.ds(hd * HEAD_DIM, HEAD_DIM)
        hn = _rms(h_scr[0, :, cols] + h_scr[1, :, cols])
        o_ref[:, cols] = (hn * nw_ref[:, cols] * jax.nn.sigmoid(og_ref[:, cols])).astype(o_ref.dtype)
    if emit_state:
        for s in range(NSTREAM):
            d, hd = divmod(s, N_HEADS)
            co_ref[d, hd] = c_scr[s][...]
            no_ref[s:s + 1, :] = n_scr[s][...]


def _mlstm(z, gates, norm_w, row0, nb, seq, init_state, layer, prev_hc, state_out):
    m_tot = z.shape[0]
    t = SCAN_CHUNK
    nsteps = seq // t
    blk0 = row0 // seq
    nrow = nsteps * NSTREAM
    g = gates.reshape(nb, nsteps, t, 2, 2, N_HEADS).transpose(0, 4, 3, 1, 5, 2)
    g = jnp.concatenate([g[:, :, 0], g[:, :, 1, ::-1]], axis=3)
    g = g.reshape(nb, 2, nrow, t)

    big = seq * WC * 4 > (2 << 20)

    def zspec(off):
        idx = lambda b: (blk0 + b, off // WC)
        return pl.BlockSpec((seq, WC), idx, pipeline_mode=pl.Buffered(1)) if big else pl.BlockSpec((seq, WC), idx)

    in_specs = [zspec(Z_QKV), zspec(Z_QKV + WC), zspec(Z_QKV + 2 * WC), zspec(Z_OG),
                pl.BlockSpec((None, 2, nrow, t), lambda b: (b, 0, 0, 0)),
                pl.BlockSpec((1, WC), lambda b: (0, 0))]
    args = [z, z, z, z, g, norm_w.reshape(1, WC)]
    if init_state is not None:
        sc, sn, sm = init_state
        depth = sc.shape[1]
        in_specs += [pl.BlockSpec((None, None, 2, N_HEADS, HEAD_DIM, HEAD_DIM), lambda b: (b, layer, 0, 0, 0, 0)),
                     pl.BlockSpec((None, None, NSTREAM, HEAD_DIM), lambda b: (b, layer, 0, 0)),
                     pl.BlockSpec((None, None, NSTREAM, 1), lambda b: (b, layer, 0, 0))]
        args += [sc, sn.reshape(nb, depth, NSTREAM, HEAD_DIM), sm.reshape(nb, depth, NSTREAM, 1)]
    out_specs = [pl.BlockSpec((seq, WC), lambda b: (blk0 + b, 0))]
    out_shape = [jax.ShapeDtypeStruct((m_tot, WC), BF16)]
    aliases = {}
    pre_args, pre_specs = [], []
    if prev_hc is not None:
        pre_args.append(prev_hc)
        pre_specs.append(pl.BlockSpec(memory_space=pl.ANY))
        aliases[0] = 0
    if state_out is not None:
        depth, prev_c = state_out
        out_specs += [pl.BlockSpec((None, None, 2, N_HEADS, HEAD_DIM, HEAD_DIM), lambda b: (b, layer, 0, 0, 0, 0)),
                      pl.BlockSpec((None, NSTREAM, HEAD_DIM), lambda b: (b, 0, 0)),
                      pl.BlockSpec((None, NSTREAM, 1), lambda b: (b, 0, 0))]
        out_shape += [jax.ShapeDtypeStruct((nb, depth, 2, N_HEADS, HEAD_DIM, HEAD_DIM), F32),
                      jax.ShapeDtypeStruct((nb, NSTREAM, HEAD_DIM), F32),
                      jax.ShapeDtypeStruct((nb, NSTREAM, 1), F32)]
        if prev_c is not None:
            aliases[len(pre_args)] = 1
            pre_args.append(prev_c)
            pre_specs.append(pl.BlockSpec(memory_space=pl.ANY))
    body = functools.partial(_mlstm_kernel, zero_init=init_state is None, emit_state=state_out is not None)
    return pl.pallas_call(
        _skip_refs(body, len(pre_args)),
        grid=(nb,),
        in_specs=pre_specs + in_specs,
        out_specs=out_specs,
        out_shape=out_shape,
        input_output_aliases=aliases,
        scratch_shapes=[pltpu.VMEM((nsteps, NSTREAM, t), F32),
                        pltpu.VMEM((nsteps, NSTREAM, HEAD_DIM), F32),
                        pltpu.VMEM((nsteps, t, LANES), F32),
                        pltpu.VMEM((2, seq, WC), F32)]
        + [pltpu.VMEM((HEAD_DIM, HEAD_DIM), F32)] * NSTREAM + [pltpu.VMEM((1, HEAD_DIM), F32)] * NSTREAM,
        compiler_params=_params("parallel"),
        name=f"mlstm_seq{seq}",
    )(*pre_args, *args)


def _merge_kernel(h_ref, a_ref, p_ref, c_ref, wga_ref, wgb_ref, wgc_ref, bga_ref, bgb_ref, bgc_ref,
                  wa_ref, wp_ref, ps_ref, wc_ref, o_ref):
    h = h_ref[...]

    def gate(w_ref, b_ref):
        return jax.nn.sigmoid(lax.dot_general(h, w_ref[...], _NT, preferred_element_type=F32) + b_ref[...])

    ya = jnp.dot(a_ref[...], wa_ref[...], preferred_element_type=F32)
    yb = jnp.dot(p_ref[...], wp_ref[...], preferred_element_type=F32) * ps_ref[...]
    yc = jnp.dot(c_ref[...], wc_ref[...], preferred_element_type=F32)
    mix = gate(wga_ref, bga_ref) * ya + gate(wgb_ref, bgb_ref) * yb + gate(wgc_ref, bgc_ref) * yc
    o_ref[...] = mix.astype(o_ref.dtype)


def _merge(h, a_act, p, hc, w_gates_t, b_gates, layer, w_a_out, w_pool, pool_scale, w_c_out):
    m, d = h.shape
    tm = 1024
    tn = d // N_POOL_GROUPS
    nj = d // tn

    def wg(k):
        return pl.BlockSpec((None, tn, d), lambda i, j: (layer, k * nj + j, 0))

    def bg(k):
        return pl.BlockSpec((None, 1, tn), lambda i, j: (layer, 0, k * nj + j))

    return pl.pallas_call(
        _merge_kernel,
        grid=(m // tm, nj),
        in_specs=[pl.BlockSpec((tm, d), lambda i, j: (i, 0)),
                  pl.BlockSpec((tm, WA), lambda i, j: (i, 0)),
                  pl.BlockSpec((tm, POOL_GC), lambda i, j: (i, j)),
                  pl.BlockSpec((tm, WC), lambda i, j: (i, 0)),
                  wg(0), wg(1), wg(2), bg(0), bg(1), bg(2),
                  pl.BlockSpec((WA, tn), lambda i, j: (0, j)),
                  pl.BlockSpec((None, POOL_GC, tn), lambda i, j: (j, 0, 0)),
                  pl.BlockSpec((1, tn), lambda i, j: (0, j)),
                  pl.BlockSpec((WC, tn), lambda i, j: (0, j))],
        out_specs=pl.BlockSpec((tm, tn), lambda i, j: (i, j)),
        out_shape=jax.ShapeDtypeStruct((m, d), BF16),
        compiler_params=_params("parallel", "arbitrary"),
        name="merge",
    )(h, a_act, p, hc, w_gates_t, w_gates_t, w_gates_t, b_gates, b_gates, b_gates,
      w_a_out, w_pool, pool_scale.reshape(1, d), w_c_out)


def _proj_residual_kernel(a_ref, w_ref, x_ref, wpost_ref, gate_ref, wpre_ref, sc_ref, sh_ref, xo_ref, ho_ref,
                          *, sub):
    for r in range(a_ref.shape[0] // sub):
        rows = pl.ds(r * sub, sub)
        y = jnp.dot(a_ref[rows, :], w_ref[...], preferred_element_type=F32)
        xn = x_ref[rows, :] + gate_ref[...] * (_rms(y) * wpost_ref[...])
        xo_ref[rows, :] = xn
        if ho_ref is not None:
            hn = _rms(xn) * wpre_ref[...]
            ho_ref[rows, :] = (hn * (1.0 + sc_ref[...]) + sh_ref[...]).astype(ho_ref.dtype)


def _proj_residual_last_kernel(a_ref, w_ref, x_ref, wpost_ref, gate_ref, xo_ref, *, sub):
    _proj_residual_kernel(a_ref, w_ref, x_ref, wpost_ref, gate_ref, None, None, None, xo_ref, None, sub=sub)


def _proj_residual(lay, a, w, x, w_post, mod, gate, tm, sub, name, nxt=None, row0=0, rows=None):
    d = lay.d
    kdim = a.shape[1]
    rows = lay.m if rows is None else rows
    blk0 = row0 // tm
    row = pl.BlockSpec((1, d), lambda i: (0, 0))
    tile = lambda width: pl.BlockSpec((tm, width), lambda i: (blk0 + i, 0))

    def mod_at(layer, k):
        spec = lay.mod_spec(layer, k, tm)
        return pl.BlockSpec(spec.block_shape, lambda i: spec.index_map(blk0 + i))

    in_specs = [tile(kdim), pl.BlockSpec((kdim, d), lambda i: (0, 0), pipeline_mode=pl.Buffered(1)), tile(d),
                row, mod_at(*gate)]
    args = [a, w, x, w_post.reshape(1, d), mod]
    out_tile = pl.BlockSpec((tm, d), lambda i: (i, 0))
    if nxt is None:
        body = functools.partial(_proj_residual_last_kernel, sub=sub)
        out_specs, out_shape = out_tile, jax.ShapeDtypeStruct((rows, d), F32)
    else:
        assert row0 == 0 and rows == lay.m
        w_pre_next, scale, shift = nxt
        body = functools.partial(_proj_residual_kernel, sub=sub)
        in_specs += [row, mod_at(*scale), mod_at(*shift)]
        args += [w_pre_next.reshape(1, d), mod, mod]
        out_specs = [out_tile, out_tile]
        out_shape = [jax.ShapeDtypeStruct((rows, d), F32), jax.ShapeDtypeStruct((rows, d), BF16)]
    return pl.pallas_call(
        body,
        grid=(rows // tm,),
        in_specs=in_specs,
        out_specs=out_specs,
        out_shape=out_shape,
        compiler_params=_params("parallel"),
        name=name,
    )(*args)


def _gelu_tanh(x):
    return 0.5 * x * (1.0 + jnp.tanh(0.7978845608028654 * (x + 0.044715 * (x * x * x))))


MXU_WIDTH = 256


def _ffn_up_kernel(h_ref, wu_ref, wg_ref, cw_ref, cb_ref, o_ref, *, shift, seg):
    h = h_ref[...]
    tm = h.shape[0]
    for c in range(o_ref.shape[1] // MXU_WIDTH):
        cols = pl.ds(c * MXU_WIDTH, MXU_WIDTH)
        u = jnp.dot(h, wu_ref[:, cols], preferred_element_type=F32)
        g = jnp.dot(h, wg_ref[:, cols], preferred_element_type=F32)
        if shift % SUBLANES == 0 and seg == tm:
            zeros = jnp.zeros((shift, MXU_WIDTH), F32)
            prev = jnp.concatenate([zeros, g[:tm - shift]], axis=0)
            nxt = jnp.concatenate([g[shift:], zeros], axis=0)
        else:
            r = lax.broadcasted_iota(jnp.int32, g.shape, 0) & (seg - 1)
            prev = jnp.where(r < shift, 0.0, pltpu.roll(g, shift, 0))
            nxt = jnp.where(r >= seg - shift, 0.0, pltpu.roll(g, tm - shift, 0))
        gc = cw_ref[0:1, cols] * prev + cw_ref[1:2, cols] * g + cw_ref[2:3, cols] * nxt + cb_ref[:, cols]
        o_ref[:, cols] = (_gelu_tanh(gc) * u).astype(o_ref.dtype)


def _ffn_up(h2, row0, rows, tm, shift, seg, w_up, layer, conv_w, conv_b, prev):
    m, d = h2.shape
    dff = w_up.shape[2] // 2
    tn = 512
    nj = dff // tn
    blk0 = row0 // tm
    pre_args, pre_specs, aliases = _into(prev)
    return pl.pallas_call(
        _skip_refs(functools.partial(_ffn_up_kernel, shift=shift, seg=seg), len(pre_args)),
        grid=(rows // tm, nj),
        in_specs=pre_specs + [pl.BlockSpec((tm, d), lambda i, j: (blk0 + i, 0)),
                              pl.BlockSpec((None, d, tn), lambda i, j: (layer, 0, j)),
                              pl.BlockSpec((None, d, tn), lambda i, j: (layer, 0, nj + j)),
                              pl.BlockSpec((3, tn), lambda i, j: (0, j)),
                              pl.BlockSpec((1, tn), lambda i, j: (0, j))],
        out_specs=pl.BlockSpec((tm, tn), lambda i, j: (blk0 + i, j)),
        out_shape=jax.ShapeDtypeStruct((m, dff), BF16),
        input_output_aliases=aliases,
        compiler_params=_params("parallel", "arbitrary"),
        name=f"ffn_up_shift{shift}",
    )(*pre_args, h2, w_up, w_up, conv_w, conv_b.reshape(1, dff))


def kernel(x_prompt, x_sample, state_C, state_n, state_m, c, c_ctx, w_ada, b_ada, norm_mix_pre, norm_mix_post, norm_ffn_pre, norm_ffn_post, w_in, b_in, conv_a_w, conv_a_b, ln_a_w, ln_a_b, w_a_out, w_pool, pool_scale, mlstm_norm_w, w_c_out, w_out, w_ffn_up, ffn_conv_w, ffn_conv_b, w_ffn_down):
    bp, lp, d = x_prompt.shape
    bs, ls, _ = x_sample.shape
    depth = w_ada.shape[0]
    lay = _Layout(bp, lp, bs, ls, d)
    assert bs < N_COND_ROWS and lay.mp % (2 * ls) == 0 and lay.ms % (2 * ls) == 0 and ls % lp == 0
    assert lp & (lp - 1) == 0 and GRID_W & (GRID_W - 1) == 0
    assert lp % SCAN_CHUNK == 0 and ls % SCAN_CHUNK == 0 and N_COLVEC * NSTREAM <= LANES

    cond = jnp.zeros((N_COND_ROWS, d), F32).at[:bs].set(c).at[bs].set(c_ctx)
    mod = _ada(cond, w_ada, b_ada).reshape(depth * N_COND_ROWS * N_MOD, 1, d)

    tc = 512
    ngate = OFF_MERGE - OFF_GATES
    assert OFF_QKV % tc == 0 and OFF_GATES % tc == 0 and ngate % 16 == 0 and ngate < tc
    nz = OFF_GATES // tc
    w_in_t = jnp.swapaxes(w_in, 1, 2)
    w_z = _cast_rows(w_in_t, lambda j: (j + OFF_QKV // tc) % nz, nz, 0, tc, "cast_w_z")
    w_m = _cast_rows(w_in_t, lambda j: nz + j, N_BRANCH * d // tc, ngate, tc, "cast_w_merge")
    gate_pad = (0, LANES - ngate)
    w_g = jnp.pad(w_in_t[:, OFF_GATES:OFF_MERGE, :], ((0, 0), gate_pad, (0, 0))).astype(BF16)
    b_z = jnp.concatenate([b_in[:, OFF_QKV:OFF_GATES], b_in[:, :OFF_QKV]], axis=-1)[:, None, :]
    b_m = b_in[:, None, OFF_MERGE:]
    b_g = jnp.pad(b_in[:, OFF_GATES:OFF_MERGE], ((0, 0), gate_pad))[:, None, :]
    w_a_out_b = w_a_out.astype(BF16)
    w_pool_b = w_pool.astype(BF16)
    w_c_out_b = w_c_out.astype(BF16)
    w_out_b = w_out.astype(BF16)
    w_up_b = w_ffn_up.astype(BF16)
    w_down_b = w_ffn_down.astype(BF16)

    x = jnp.concatenate([x_prompt.reshape(lay.mp, d), x_sample.reshape(lay.ms, d)], axis=0)
    h = _norm_mod(lay, x, norm_mix_pre[0], mod, 0, 1, 0)
    new_c = None
    new_n, new_m = [], []
    for l in range(depth):
        z = _matmul_bias(h, w_z, b_z, l, Z_WIDTH, 2048, 512, "in_proj")
        zg = _matmul_bias(h, w_g, b_g, l, LANES, 1024, LANES, "gate_proj")[:, :4 * N_HEADS]

        conv_args = (conv_a_w[l], conv_a_b[l], ln_a_w[l], ln_a_b[l])
        a_act = _conv_a(z, 0, lay.mp, lp, *conv_args, None)
        a_act = _conv_a(z, lay.mp, lay.ms, GRID_W, *conv_args, a_act)
        p = _pool(z, 0, lay.mp, lp, 1, None)
        p = _pool(z, lay.mp, lay.ms, ls, GRID_W, p)
        hc, new_c, n_l, m_l = _mlstm(z, zg[:lay.mp], mlstm_norm_w[l], 0, bp, lp, None, l, None, (depth, new_c))
        (hc,) = _mlstm(z, zg[lay.mp:], mlstm_norm_w[l], lay.mp, bs, ls, (state_C, state_n, state_m), l, hc, None)
        new_n.append(n_l.reshape(bp, 2, N_HEADS, HEAD_DIM))
        new_m.append(m_l.reshape(bp, 2, N_HEADS))

        mix_in = _merge(h, a_act, p, hc, w_m, b_m, l, w_a_out_b[l], w_pool_b[l], pool_scale[l], w_c_out_b[l])
        x, h2 = _proj_residual(lay, mix_in, w_out_b[l], x, norm_mix_post[l], mod, (l, 2), 512, 256, "out_proj",
                               nxt=(norm_ffn_pre[l], (l, 4), (l, 3)))
        ffn_args = (w_up_b, l, ffn_conv_w[l], ffn_conv_b[l])
        tm_ffn = ls
        f_in = _ffn_up(h2, 0, lay.mp, tm_ffn, 1, lp, *ffn_args, None)
        f_in = _ffn_up(h2, lay.mp, lay.ms, tm_ffn, GRID_W, ls, *ffn_args, f_in)
        down_args = (lay, f_in, w_down_b[l], x, norm_ffn_post[l], mod, (l, 5), 256, 128, "ffn_down")
        if l + 1 < depth:
            x, h = _proj_residual(*down_args, nxt=(norm_mix_pre[l + 1], (l + 1, 1), (l + 1, 0)))
        else:
            y_prompt = _proj_residual(*down_args, row0=0, rows=lay.mp).reshape(bp, lp, d)
            y_sample = _proj_residual(*down_args, row0=lay.mp, rows=lay.ms).reshape(bs, ls, d)
    return (y_prompt, y_sample, new_c, jnp.stack(new_n, axis=1), jnp.stack(new_m, axis=1))
```

```python
import functools

import jax
import jax.numpy as jnp
from jax import lax
from jax.experimental import pallas as pl
from jax.experimental.pallas import tpu as pltpu

GRID_W = 64
WA = 512
WB = 512
N_POOL_GROUPS = 4
POOL_GC = WB // N_POOL_GROUPS
POOL_WINDOWS = (2, 4, 8, 16)
WC = 1024
N_HEADS = 4
HEAD_DIM = WC // N_HEADS
CONV_A_WIDTH = 31
CHUNK = 64
N_BRANCH = 3
N_MOD = 6
EPS = 1e-6

OFF_POOL = 2 * WA
OFF_QKV = OFF_POOL + WB
OFF_OG = OFF_QKV + 3 * WC
OFF_GATES = OFF_OG + WC
OFF_MERGE = OFF_GATES + 4 * N_HEADS

Z_QKV = 0
Z_OG = Z_QKV + 3 * WC
Z_GLU = Z_OG + WC
Z_POOL = Z_GLU + 2 * WA
Z_WIDTH = Z_POOL + WB

LANES = 128
SUBLANES = 8
VMEM_LIMIT_BYTES = 56 * 1024 * 1024

N_COND_ROWS = 8
BF16 = jnp.bfloat16
F32 = jnp.float32


def _params(*sem):
    return pltpu.CompilerParams(dimension_semantics=sem, vmem_limit_bytes=VMEM_LIMIT_BYTES)


def _rms(x):
    return x * lax.rsqrt(jnp.mean(x * x, axis=-1, keepdims=True) + EPS)


def _ada_kernel(c_ref, w_ref, b_ref, o_ref):
    c = c_ref[...]
    a = c * jax.nn.sigmoid(c)
    o_ref[...] = jnp.dot(a, w_ref[...], preferred_element_type=F32) + b_ref[...]


def _ada(cond, w_ada, b_ada):
    depth, d, n = w_ada.shape
    tn = 1024
    return pl.pallas_call(
        _ada_kernel,
        grid=(depth, n // tn),
        in_specs=[pl.BlockSpec((N_COND_ROWS, d), lambda l, j: (0, 0)),
                  pl.BlockSpec((None, d, tn), lambda l, j: (l, 0, j)),
                  pl.BlockSpec((None, 1, tn), lambda l, j: (l, 0, j))],
        out_specs=pl.BlockSpec((None, N_COND_ROWS, tn), lambda l, j: (l, 0, j)),
        out_shape=jax.ShapeDtypeStruct((depth, N_COND_ROWS, n), F32),
        compiler_params=_params("arbitrary", "arbitrary"),
        name="ada",
    )(cond, w_ada, b_ada.reshape(depth, 1, n))


class _Layout:
    def __init__(self, bp, lp, bs, ls, d):
        self.bp, self.lp, self.bs, self.ls, self.d = bp, lp, bs, ls, d
        self.mp = bp * lp
        self.ms = bs * ls
        self.m = self.mp + self.ms

    def mod_spec(self, layer, k, tm):
        mp, ls, bs = self.mp, self.ls, self.bs

        def index(i, *_):
            start = i * tm
            row = jnp.where(start < mp, bs, (start - mp) // ls)
            return ((layer * N_COND_ROWS + row) * N_MOD + k, 0, 0)

        return pl.BlockSpec((None, 1, self.d), index)


def _gate_preact(h, wg_ref, bg_ref):
    return lax.dot_general(h, wg_ref[...], _NT, preferred_element_type=F32) + bg_ref[...]


def _entry_kernel(xp_ref, xs_ref, w_ref, sc_ref, sh_ref, wg_ref, bg_ref, xo_ref, ho_ref, zg_ref, *, n_ctx_tiles):
    x = jnp.where(pl.program_id(0) < n_ctx_tiles, xp_ref[...], xs_ref[...])
    xo_ref[...] = x
    h = ((_rms(x) * w_ref[...]) * (1.0 + sc_ref[...]) + sh_ref[...]).astype(ho_ref.dtype)
    ho_ref[...] = h
    zg_ref[...] = _gate_preact(h, wg_ref, bg_ref)


def _entry(lay, xp, xs, w, mod, w_g, b_g):
    tm = 512
    d = lay.d
    n_ctx = lay.mp // tm
    n_lat = lay.ms // tm
    tile = pl.BlockSpec((tm, d), lambda i: (i, 0))
    return pl.pallas_call(
        functools.partial(_entry_kernel, n_ctx_tiles=n_ctx),
        grid=(n_ctx + n_lat,),
        in_specs=[pl.BlockSpec((tm, d), lambda i: (jnp.minimum(i, n_ctx - 1), 0)),
                  pl.BlockSpec((tm, d), lambda i: (jnp.maximum(i - n_ctx, 0), 0)),
                  pl.BlockSpec((1, d), lambda i: (0, 0)),
                  lay.mod_spec(0, 1, tm), lay.mod_spec(0, 0, tm),
                  pl.BlockSpec((None, LANES, d), lambda i: (0, 0, 0)),
                  pl.BlockSpec((None, 1, LANES), lambda i: (0, 0, 0))],
        out_specs=[tile, tile, pl.BlockSpec((tm, LANES), lambda i: (i, 0))],
        out_shape=[jax.ShapeDtypeStruct((lay.m, d), F32), jax.ShapeDtypeStruct((lay.m, d), BF16),
                   jax.ShapeDtypeStruct((lay.m, LANES), F32)],
        compiler_params=_params("parallel"),
        name="entry",
    )(xp, xs, w.reshape(1, d), mod, mod, w_g, b_g)


_NT = (((1,), (1,)), ((), ()))


def _matmul_bias_kernel(a_ref, w_ref, b_ref, o_ref):
    o_ref[...] = lax.dot_general(a_ref[...], w_ref[...], _NT, preferred_element_type=F32) + b_ref[...]


def _matmul_bias(a, wt, b, layer, n, tm, tn, name):
    m, k = a.shape
    return pl.pallas_call(
        _matmul_bias_kernel,
        grid=(m // tm, n // tn),
        in_specs=[pl.BlockSpec((tm, k), lambda i, j: (i, 0)),
                  pl.BlockSpec((None, tn, k), lambda i, j: (layer, j, 0)),
                  pl.BlockSpec((None, 1, tn), lambda i, j: (layer, 0, j))],
        out_specs=pl.BlockSpec((tm, tn), lambda i, j: (i, j)),
        out_shape=jax.ShapeDtypeStruct((m, n), F32),
        compiler_params=_params("parallel", "arbitrary"),
        name=name,
    )(a, wt, b)


def _cast_rows_kernel(*refs, shift):
    o_ref = refs[-1]
    tn = o_ref.shape[0]
    if shift == 0:
        o_ref[...] = refs[0][...].astype(o_ref.dtype)
    else:
        o_ref[pl.ds(0, tn - shift), :] = refs[0][pl.ds(shift, tn - shift), :].astype(o_ref.dtype)
        o_ref[pl.ds(tn - shift, shift), :] = refs[1][pl.ds(0, shift), :].astype(o_ref.dtype)


def _cast_rows(wt, src_block, nblocks, shift, tn, name):
    depth, n_src, k = wt.shape
    in_specs = [pl.BlockSpec((None, tn, k), lambda l, j: (l, src_block(j), 0))]
    if shift:
        assert tn % shift == 0 and n_src % shift == 0
        in_specs.append(pl.BlockSpec((None, shift, k), lambda l, j: (l, (src_block(j) + 1) * (tn // shift), 0)))
    return pl.pallas_call(
        functools.partial(_cast_rows_kernel, shift=shift),
        grid=(depth, nblocks),
        in_specs=in_specs,
        out_specs=pl.BlockSpec((None, tn, k), lambda l, j: (l, j, 0)),
        out_shape=jax.ShapeDtypeStruct((depth, nblocks * tn, k), BF16),
        compiler_params=_params("parallel", "parallel"),
        name=name,
    )(*([wt] * len(in_specs)))


def _into(prev):
    if prev is None:
        return [], [], {}
    return [prev], [pl.BlockSpec(memory_space=pl.ANY)], {0: 0}


def _skip_refs(kernel_fn, n):
    if n == 0:
        return kernel_fn
    return lambda *refs: kernel_fn(*refs[n:])


CONV_PAD = 16
CONV_ROWS = 32


def _conv_a_kernel(z_ref, w_ref, b_ref, lw_ref, lb_ref, o_ref, pad_ref, shifted_ref, *, seg):
    t = z_ref.shape[0]
    stride = seg + 2 * CONV_PAD
    half = CONV_A_WIDTH // 2
    zeros = jnp.zeros((CONV_PAD, WA), F32)
    for s in range(t // seg):
        base = s * stride
        z = z_ref[pl.ds(s * seg, seg), :]
        pad_ref[pl.ds(base, CONV_PAD), :] = zeros
        pad_ref[pl.ds(base + CONV_PAD, seg), :] = z[:, :WA] * jax.nn.sigmoid(z[:, WA:])
        pad_ref[pl.ds(base + CONV_PAD + seg, CONV_PAD), :] = zeros
    nshift = pad_ref.shape[0] - SUBLANES
    for rho in range(1, SUBLANES):
        shifted_ref[rho - 1, pl.ds(0, nshift), :] = pad_ref[pl.ds(rho, nshift), :]
    for s in range(t // seg):
        for c in range(seg // CONV_ROWS):
            row0 = s * stride + CONV_PAD + c * CONV_ROWS
            acc = jnp.broadcast_to(b_ref[...], (CONV_ROWS, WA))
            for tap in range(CONV_A_WIDTH):
                rho = (tap - half) % SUBLANES
                rows = pl.ds(row0 + tap - half - rho, CONV_ROWS)
                window = pad_ref[rows, :] if rho == 0 else shifted_ref[rho - 1, rows, :]
                acc = acc + window * w_ref[pl.ds(tap, 1), :]
            mu = jnp.mean(acc, axis=-1, keepdims=True)
            cen = acc - mu
            var = jnp.mean(cen * cen, axis=-1, keepdims=True)
            y = cen * lax.rsqrt(var + EPS) * lw_ref[...] + lb_ref[...]
            y = y * jax.nn.sigmoid(y)
            o_ref[pl.ds(s * seg + c * CONV_ROWS, CONV_ROWS), :] = y.astype(o_ref.dtype)


def _conv_a(z, row0, rows, seg, conv_w, conv_b, ln_w, ln_b, prev):
    t = 256
    nseg = t // seg
    blk0 = row0 // t
    pre_args, pre_specs, aliases = _into(prev)
    return pl.pallas_call(
        _skip_refs(functools.partial(_conv_a_kernel, seg=seg), len(pre_args)),
        grid=(rows // t,),
        in_specs=pre_specs + [pl.BlockSpec((t, 2 * WA), lambda i: (blk0 + i, Z_GLU // (2 * WA))),
                              pl.BlockSpec((CONV_A_WIDTH, WA), lambda i: (0, 0)),
                              pl.BlockSpec((1, WA), lambda i: (0, 0)),
                              pl.BlockSpec((1, WA), lambda i: (0, 0)),
                              pl.BlockSpec((1, WA), lambda i: (0, 0))],
        out_specs=pl.BlockSpec((t, WA), lambda i: (blk0 + i, 0)),
        out_shape=jax.ShapeDtypeStruct((z.shape[0], WA), BF16),
        input_output_aliases=aliases,
        scratch_shapes=[pltpu.VMEM((nseg * (seg + 2 * CONV_PAD), WA), F32),
                        pltpu.VMEM((SUBLANES - 1, nseg * (seg + 2 * CONV_PAD), WA), F32)],
        compiler_params=_params("parallel"),
        name=f"conv_a_seg{seg}",
    )(*pre_args, z, conv_w, conv_b.reshape(1, WA), ln_w.reshape(1, WA), ln_b.reshape(1, WA))


POOL_ROWS = 64
POOL_HALF_MAX = max(POOL_WINDOWS) // 2


def _pool_kernel(z_ref, o_ref, pad_ref, *, dil):
    t = z_ref.shape[0]
    npos = t // dil
    padr = POOL_HALF_MAX * dil
    zeros = jnp.zeros((padr, WB), F32)
    pad_ref[pl.ds(0, padr), :] = zeros
    pad_ref[pl.ds(padr, t), :] = z_ref[...]
    pad_ref[pl.ds(padr + t, padr), :] = zeros
    for c in range(t // POOL_ROWS):
        r0 = c * POOL_ROWS
        pos = (r0 + lax.broadcasted_iota(jnp.int32, (POOL_ROWS, POOL_GC), 0)) >> (dil.bit_length() - 1)
        for g, win in enumerate(POOL_WINDOWS):
            lanes = pl.ds(g * POOL_GC, POOL_GC)
            acc = None
            for o in range(-(win // 2), win - win // 2):
                term = pad_ref[pl.ds(padr + r0 + o * dil, POOL_ROWS), lanes]
                acc = term if acc is None else acc + term
            lo = jnp.maximum(pos - win // 2, 0)
            hi = jnp.minimum(pos - win // 2 + win, npos)
            cnt = (hi - lo).astype(F32)
            x = pad_ref[pl.ds(padr + r0, POOL_ROWS), lanes]
            o_ref[pl.ds(r0, POOL_ROWS), lanes] = (acc / cnt - x).astype(o_ref.dtype)


def _pool(z, row0, rows, t, dil, prev):
    blk0 = row0 // t
    pre_args, pre_specs, aliases = _into(prev)
    return pl.pallas_call(
        _skip_refs(functools.partial(_pool_kernel, dil=dil), len(pre_args)),
        grid=(rows // t,),
        in_specs=pre_specs + [pl.BlockSpec((t, WB), lambda i: (blk0 + i, Z_POOL // WB))],
        out_specs=pl.BlockSpec((t, WB), lambda i: (blk0 + i, 0)),
        out_shape=jax.ShapeDtypeStruct((z.shape[0], WB), BF16),
        input_output_aliases=aliases,
        scratch_shapes=[pltpu.VMEM((t + 2 * POOL_HALF_MAX * dil, WB), F32)],
        compiler_params=_params("parallel"),
        name=f"pool_dil{dil}",
    )(*pre_args, z)


SCAN_CHUNK = 256
NSTREAM = 2 * N_HEADS
N_COLVEC = 4


def _log_sigmoid(x):
    return jnp.minimum(x, 0.0) - jnp.log1p(jnp.exp(-jnp.abs(x)))


def _lane_scan(x, op, ident, reverse):
    lane = lax.broadcasted_iota(jnp.int32, x.shape, 1)
    n = x.shape[1]
    sh = 1
    while sh < n:
        if reverse:
            x = op(x, jnp.where(lane < n - sh, pltpu.roll(x, n - sh, 1), ident))
        else:
            x = op(x, jnp.where(lane >= sh, pltpu.roll(x, sh, 1), ident))
        sh *= 2
    return x


def _mlstm_kernel(*refs, zero_init, emit_state):
    q_ref, k_ref, v_ref, og_ref, g_ref, nw_ref = refs[:6]
    pos = 6
    if not zero_init:
        c0_ref, n0_ref, m0_ref = refs[pos:pos + 3]
        pos += 3
    o_ref = refs[pos]
    pos += 1
    if emit_state:
        co_ref, no_ref, mo_ref = refs[pos:pos + 3]
        pos += 3
    rowbuf, decbuf, colbuf, h_scr = refs[pos:pos + 4]
    c_scr = refs[pos + 4:pos + 4 + NSTREAM]
    n_scr = refs[pos + 4 + NSTREAM:pos + 4 + 2 * NSTREAM]

    t = SCAN_CHUNK
    nsteps = q_ref.shape[0] // t

    log_i = g_ref[0]
    log_f = _log_sigmoid(g_ref[1])
    backward = (lax.broadcasted_iota(jnp.int32, log_i.shape, 0) & N_HEADS) != 0

    def scan(x, op, ident):
        return jnp.where(backward, _lane_scan(x, op, ident, True), _lane_scan(x, op, ident, False))

    b = scan(log_f, jnp.add, 0.0)
    c = log_i - b
    c_run = scan(c, jnp.maximum, -jnp.inf)
    b_tot = jnp.sum(log_f, axis=1, keepdims=True)
    c_tot = jnp.max(c, axis=1, keepdims=True)
    m = jnp.zeros((NSTREAM, 1), F32) if zero_init else m0_ref[...]
    unused_rows = jnp.zeros((LANES - N_COLVEC * NSTREAM, t), F32)
    for i in range(nsteps):
        sl = slice(i * NSTREAM, (i + 1) * NSTREAM)
        m_last = jnp.maximum(m, c_tot[sl])
        m_run = jnp.maximum(m, c_run[sl])
        packed = [m_run, jnp.exp(m - m_run), jnp.exp(-(b[sl] + m_run)), jnp.exp(c[sl] - m_last), unused_rows]
        colbuf[i] = jnp.concatenate(packed, axis=0).T
        rowbuf[i] = c[sl]
        decbuf[i] = jnp.broadcast_to(jnp.exp(m - m_last), (NSTREAM, HEAD_DIM))
        m = b_tot[sl] + m_last
    if emit_state:
        mo_ref[...] = m

    if not zero_init:
        for s in range(NSTREAM):
            d, hd = divmod(s, N_HEADS)
            c_scr[s][...] = c0_ref[d, hd]
            n_scr[s][...] = n0_ref[s:s + 1, :]

    t_idx = lax.broadcasted_iota(jnp.int32, (t, t), 0)
    s_idx = lax.broadcasted_iota(jnp.int32, (t, t), 1)
    scanned = (s_idx <= t_idx, s_idx >= t_idx)

    def step(i, has_state, update_state):
        col = colbuf[i]

        def colvec(vec, s):
            return col[:, vec * NSTREAM + s:vec * NSTREAM + s + 1]

        def where(s):
            d, hd = divmod(s, N_HEADS)
            cidx = i if d == 0 else nsteps - 1 - i
            return d, pl.ds(pl.multiple_of(cidx * t, t), t), pl.ds(hd * HEAD_DIM, HEAD_DIM)

        scores = []
        for s in range(NSTREAM):
            d, rows, cols = where(s)
            dmat = jnp.exp(jnp.where(scanned[d], rowbuf[i, s:s + 1, :] - colvec(0, s), -jnp.inf))
            q = q_ref[rows, cols] * (HEAD_DIM ** -0.5)
            scores.append(lax.dot_general(q, k_ref[rows, cols], (((1,), (1,)), ((), ())),
                                          preferred_element_type=F32) * dmat)
        for s in range(NSTREAM):
            d, rows, cols = where(s)
            num = jnp.dot(scores[s], v_ref[rows, cols], preferred_element_type=F32)
            den = jnp.sum(scores[s], axis=1, keepdims=True)
            if has_state:
                q = q_ref[rows, cols] * (HEAD_DIM ** -0.5)
                w_int = colvec(1, s)
                num = num + w_int * jnp.dot(q, c_scr[s][...], preferred_element_type=F32)
                den = den + w_int * jnp.sum(q * n_scr[s][...], axis=1, keepdims=True)
            h_scr[d, rows, cols] = num * (1.0 / jnp.maximum(jnp.abs(den), colvec(2, s)))
        if update_state:
            for s in range(NSTREAM):
                d, rows, cols = where(s)
                wk = k_ref[rows, cols] * colvec(3, s)
                c_new = lax.dot_general(wk, v_ref[rows, cols], (((0,), (0,)), ((), ())),
                                        preferred_element_type=F32)
                n_new = jnp.sum(wk, axis=0, keepdims=True)
                if has_state:
                    dec = decbuf[i, s:s + 1, :]
                    c_new = c_new + dec * c_scr[s][...]
                    n_new = n_new + dec * n_scr[s][...]
                c_scr[s][...] = c_new
                n_scr[s][...] = n_new

    first = 1 if zero_init else 0
    last = nsteps if emit_state else nsteps - 1
    if zero_init:
        step(0, False, emit_state or nsteps > 1)
    if last - first == 1:
        step(first, True, True)
    elif last > first:
        def body(i, carry):
            step(i, True, True)
            return carry
        lax.fori_loop(first, last, body, 0)
    if not emit_state and not (zero_init and nsteps == 1):
        step(nsteps - 1, True, False)

    for hd in range(N_HEADS):
        cols = pl.ds(hd * HEAD_DIM, HEAD_DIM)
        hn = _rms(h_scr[0, :, cols] + h_scr[1, :, cols])
        o_ref[:, cols] = (hn * nw_ref[:, cols] * jax.nn.sigmoid(og_ref[:, cols])).astype(o_ref.dtype)
    if emit_state:
        for s in range(NSTREAM):
            d, hd = divmod(s, N_HEADS)
            co_ref[d, hd] = c_scr[s][...]
            no_ref[s:s + 1, :] = n_scr[s][...]


def _mlstm(z, gates, norm_w, row0, nb, seq, init_state, layer, prev_hc, state_out):
    m_tot = z.shape[0]
    t = SCAN_CHUNK
    nsteps = seq // t
    blk0 = row0 // seq
    nrow = nsteps * NSTREAM
    g = gates.reshape(nb, nsteps, t, 2, 2, N_HEADS).transpose(0, 4, 3, 1, 5, 2)
    g = jnp.concatenate([g[:, :, 0], g[:, :, 1, ::-1]], axis=3)
    g = g.reshape(nb, 2, nrow, t)

    big = seq * WC * 4 > (2 << 20)

    def zspec(off):
        idx = lambda b: (blk0 + b, off // WC)
        return pl.BlockSpec((seq, WC), idx, pipeline_mode=pl.Buffered(1)) if big else pl.BlockSpec((seq, WC), idx)

    in_specs = [zspec(Z_QKV), zspec(Z_QKV + WC), zspec(Z_QKV + 2 * WC), zspec(Z_OG),
                pl.BlockSpec((None, 2, nrow, t), lambda b: (b, 0, 0, 0)),
                pl.BlockSpec((1, WC), lambda b: (0, 0))]
    args = [z, z, z, z, g, norm_w.reshape(1, WC)]
    if init_state is not None:
        sc, sn, sm = init_state
        depth = sc.shape[1]
        in_specs += [pl.BlockSpec((None, None, 2, N_HEADS, HEAD_DIM, HEAD_DIM), lambda b: (b, layer, 0, 0, 0, 0)),
                     pl.BlockSpec((None, None, NSTREAM, HEAD_DIM), lambda b: (b, layer, 0, 0)),
                     pl.BlockSpec((None, None, NSTREAM, 1), lambda b: (b, layer, 0, 0))]
        args += [sc, sn.reshape(nb, depth, NSTREAM, HEAD_DIM), sm.reshape(nb, depth, NSTREAM, 1)]
    out_specs = [pl.BlockSpec((seq, WC), lambda b: (blk0 + b, 0))]
    out_shape = [jax.ShapeDtypeStruct((m_tot, WC), BF16)]
    aliases = {}
    pre_args, pre_specs = [], []
    if prev_hc is not None:
        pre_args.append(prev_hc)
        pre_specs.append(pl.BlockSpec(memory_space=pl.ANY))
        aliases[0] = 0
    if state_out is not None:
        depth, prev_c = state_out
        out_specs += [pl.BlockSpec((None, None, 2, N_HEADS, HEAD_DIM, HEAD_DIM), lambda b: (b, layer, 0, 0, 0, 0)),
                      pl.BlockSpec((None, NSTREAM, HEAD_DIM), lambda b: (b, 0, 0)),
                      pl.BlockSpec((None, NSTREAM, 1), lambda b: (b, 0, 0))]
        out_shape += [jax.ShapeDtypeStruct((nb, depth, 2, N_HEADS, HEAD_DIM, HEAD_DIM), F32),
                      jax.ShapeDtypeStruct((nb, NSTREAM, HEAD_DIM), F32),
                      jax.ShapeDtypeStruct((nb, NSTREAM, 1), F32)]
        if prev_c is not None:
            aliases[len(pre_args)] = 1
            pre_args.append(prev_c)
            pre_specs.append(pl.BlockSpec(memory_space=pl.ANY))
    body = functools.partial(_mlstm_kernel, zero_init=init_state is None, emit_state=state_out is not None)
    return pl.pallas_call(
        _skip_refs(body, len(pre_args)),
        grid=(nb,),
        in_specs=pre_specs + in_specs,
        out_specs=out_specs,
        out_shape=out_shape,
        input_output_aliases=aliases,
        scratch_shapes=[pltpu.VMEM((nsteps, NSTREAM, t), F32),
                        pltpu.VMEM((nsteps, NSTREAM, HEAD_DIM), F32),
                        pltpu.VMEM((nsteps, t, LANES), F32),
                        pltpu.VMEM((2, seq, WC), F32)]
        + [pltpu.VMEM((HEAD_DIM, HEAD_DIM), F32)] * NSTREAM + [pltpu.VMEM((1, HEAD_DIM), F32)] * NSTREAM,
        compiler_params=_params("parallel"),
        name=f"mlstm_seq{seq}",
    )(*pre_args, *args)


def _merge_kernel(h_ref, a_ref, p_ref, c_ref, wga_ref, wgb_ref, wgc_ref, bga_ref, bgb_ref, bgc_ref,
                  wa_ref, wp_ref, ps_ref, wc_ref, o_ref):
    h = h_ref[...]

    def gate(w_ref, b_ref):
        return jax.nn.sigmoid(lax.dot_general(h, w_ref[...], _NT, preferred_element_type=F32) + b_ref[...])

    ya = jnp.dot(a_ref[...], wa_ref[...], preferred_element_type=F32)
    yb = jnp.dot(p_ref[...], wp_ref[...], preferred_element_type=F32) * ps_ref[...]
    yc = jnp.dot(c_ref[...], wc_ref[...], preferred_element_type=F32)
    mix = gate(wga_ref, bga_ref) * ya + gate(wgb_ref, bgb_ref) * yb + gate(wgc_ref, bgc_ref) * yc
    o_ref[...] = mix.astype(o_ref.dtype)


def _merge(h, a_act, p, hc, w_gates_t, b_gates, layer, w_a_out, w_pool, pool_scale, w_c_out):
    m, d = h.shape
    tm = 1024
    tn = d // N_POOL_GROUPS
    nj = d // tn

    def wg(k):
        return pl.BlockSpec((None, tn, d), lambda i, j: (layer, k * nj + j, 0))

    def bg(k):
        return pl.BlockSpec((None, 1, tn), lambda i, j: (layer, 0, k * nj + j))

    return pl.pallas_call(
        _merge_kernel,
        grid=(m // tm, nj),
        in_specs=[pl.BlockSpec((tm, d), lambda i, j: (i, 0)),
                  pl.BlockSpec((tm, WA), lambda i, j: (i, 0)),
                  pl.BlockSpec((tm, POOL_GC), lambda i, j: (i, j)),
                  pl.BlockSpec((tm, WC), lambda i, j: (i, 0)),
                  wg(0), wg(1), wg(2), bg(0), bg(1), bg(2),
                  pl.BlockSpec((None, WA, tn), lambda i, j: (layer, 0, j)),
                  pl.BlockSpec((None, None, POOL_GC, tn), lambda i, j: (layer, j, 0, 0)),
                  pl.BlockSpec((1, tn), lambda i, j: (0, j)),
                  pl.BlockSpec((None, WC, tn), lambda i, j: (layer, 0, j))],
        out_specs=pl.BlockSpec((tm, tn), lambda i, j: (i, j)),
        out_shape=jax.ShapeDtypeStruct((m, d), BF16),
        compiler_params=_params("parallel", "arbitrary"),
        name="merge",
    )(h, a_act, p, hc, w_gates_t, w_gates_t, w_gates_t, b_gates, b_gates, b_gates,
      w_a_out, w_pool, pool_scale.reshape(1, d), w_c_out)


def _proj_residual_kernel(*refs, sub, emit_next):
    a_ref, w_ref, x_ref, wpost_ref, gate_ref = refs[:5]
    if emit_next:
        wpre_ref, sc_ref, sh_ref, xo_ref, ho_ref = refs[5:]
    else:
        (xo_ref,) = refs[5:]
    for r in range(a_ref.shape[0] // sub):
        rows = pl.ds(r * sub, sub)
        y = jnp.dot(a_ref[rows, :], w_ref[...], preferred_element_type=F32)
        xn = x_ref[rows, :] + gate_ref[...] * (_rms(y) * wpost_ref[...])
        xo_ref[rows, :] = xn
        if emit_next:
            hn = _rms(xn) * wpre_ref[...]
            ho_ref[rows, :] = (hn * (1.0 + sc_ref[...]) + sh_ref[...]).astype(ho_ref.dtype)


def _proj_residual(lay, a, w, layer, x, w_post, mod, gate, tm, sub, name, nxt=None, row0=0, rows=None):
    d = lay.d
    kdim = a.shape[1]
    rows = lay.m if rows is None else rows
    blk0 = row0 // tm
    row = pl.BlockSpec((1, d), lambda i: (0, 0))
    tile = lambda width: pl.BlockSpec((tm, width), lambda i: (blk0 + i, 0))

    def mod_at(layer_k, k):
        spec = lay.mod_spec(layer_k, k, tm)
        return pl.BlockSpec(spec.block_shape, lambda i: spec.index_map(blk0 + i))

    in_specs = [tile(kdim),
                pl.BlockSpec((None, kdim, d), lambda i: (layer, 0, 0), pipeline_mode=pl.Buffered(1)),
                tile(d), row, mod_at(*gate)]
    args = [a, w, x, w_post.reshape(1, d), mod]
    out_tile = pl.BlockSpec((tm, d), lambda i: (i, 0))
    out_specs, out_shape = [out_tile], [jax.ShapeDtypeStruct((rows, d), F32)]
    if nxt is not None:
        assert row0 == 0 and rows == lay.m
        w_pre_next, scale, shift = nxt
        in_specs += [row, mod_at(*scale), mod_at(*shift)]
        args += [w_pre_next.reshape(1, d), mod, mod]
        out_specs.append(out_tile)
        out_shape.append(jax.ShapeDtypeStruct((rows, d), BF16))
    body = functools.partial(_proj_residual_kernel, sub=sub, emit_next=nxt is not None)
    return pl.pallas_call(
        body,
        grid=(rows // tm,),
        in_specs=in_specs,
        out_specs=out_specs,
        out_shape=out_shape,
        compiler_params=_params("parallel"),
        name=name,
    )(*args)


def _gelu_tanh(x):
    return 0.5 * x * (1.0 + jnp.tanh(0.7978845608028654 * (x + 0.044715 * (x * x * x))))


MXU_WIDTH = 256


def _ffn_up_kernel(h_ref, wu_ref, wg_ref, cw_ref, cb_ref, o_ref, *, shift, seg):
    h = h_ref[...]
    tm = h.shape[0]
    for c in range(o_ref.shape[1] // MXU_WIDTH):
        cols = pl.ds(c * MXU_WIDTH, MXU_WIDTH)
        u = jnp.dot(h, wu_ref[:, cols], preferred_element_type=F32)
        g = jnp.dot(h, wg_ref[:, cols], preferred_element_type=F32)
        if shift % SUBLANES == 0 and seg == tm:
            zeros = jnp.zeros((shift, MXU_WIDTH), F32)
            prev = jnp.concatenate([zeros, g[:tm - shift]], axis=0)
            nxt = jnp.concatenate([g[shift:], zeros], axis=0)
        else:
            r = lax.broadcasted_iota(jnp.int32, g.shape, 0) & (seg - 1)
            prev = jnp.where(r < shift, 0.0, pltpu.roll(g, shift, 0))
            nxt = jnp.where(r >= seg - shift, 0.0, pltpu.roll(g, tm - shift, 0))
        gc = cw_ref[0:1, cols] * prev + cw_ref[1:2, cols] * g + cw_ref[2:3, cols] * nxt + cb_ref[:, cols]
        o_ref[:, cols] = (_gelu_tanh(gc) * u).astype(o_ref.dtype)


def _ffn_up(h2, row0, rows, tm, shift, seg, w_up, layer, conv_w, conv_b, prev):
    m, d = h2.shape
    dff = w_up.shape[2] // 2
    tn = 512
    nj = dff // tn
    blk0 = row0 // tm
    pre_args, pre_specs, aliases = _into(prev)
    return pl.pallas_call(
        _skip_refs(functools.partial(_ffn_up_kernel, shift=shift, seg=seg), len(pre_args)),
        grid=(rows // tm, nj),
        in_specs=pre_specs + [pl.BlockSpec((tm, d), lambda i, j: (blk0 + i, 0)),
                              pl.BlockSpec((None, d, tn), lambda i, j: (layer, 0, j)),
                              pl.BlockSpec((None, d, tn), lambda i, j: (layer, 0, nj + j)),
                              pl.BlockSpec((3, tn), lambda i, j: (0, j)),
                              pl.BlockSpec((1, tn), lambda i, j: (0, j))],
        out_specs=pl.BlockSpec((tm, tn), lambda i, j: (blk0 + i, j)),
        out_shape=jax.ShapeDtypeStruct((m, dff), BF16),
        input_output_aliases=aliases,
        compiler_params=_params("parallel", "arbitrary"),
        name=f"ffn_up_shift{shift}",
    )(*pre_args, h2, w_up, w_up, conv_w, conv_b.reshape(1, dff))


def kernel(x_prompt, x_sample, state_C, state_n, state_m, c, c_ctx, w_ada, b_ada, norm_mix_pre, norm_mix_post, norm_ffn_pre, norm_ffn_post, w_in, b_in, conv_a_w, conv_a_b, ln_a_w, ln_a_b, w_a_out, w_pool, pool_scale, mlstm_norm_w, w_c_out, w_out, w_ffn_up, ffn_conv_w, ffn_conv_b, w_ffn_down):
    bp, lp, d = x_prompt.shape
    bs, ls, _ = x_sample.shape
    depth = w_ada.shape[0]
    lay = _Layout(bp, lp, bs, ls, d)
    assert bs < N_COND_ROWS and lay.mp % (2 * ls) == 0 and lay.ms % (2 * ls) == 0 and ls % lp == 0
    assert lp & (lp - 1) == 0 and GRID_W & (GRID_W - 1) == 0
    assert lp % SCAN_CHUNK == 0 and ls % SCAN_CHUNK == 0 and N_COLVEC * NSTREAM <= LANES

    cond = jnp.zeros((N_COND_ROWS, d), F32).at[:bs].set(c).at[bs].set(c_ctx)
    mod = _ada(cond, w_ada, b_ada).reshape(depth * N_COND_ROWS * N_MOD, 1, d)

    tc = 512
    ngate = OFF_MERGE - OFF_GATES
    assert OFF_QKV % tc == 0 and OFF_GATES % tc == 0 and ngate % 16 == 0 and ngate < tc
    nz = OFF_GATES // tc
    w_in_t = jnp.swapaxes(w_in, 1, 2)
    w_z = _cast_rows(w_in_t, lambda j: (j + OFF_QKV // tc) % nz, nz, 0, tc, "cast_w_z")
    w_m =_cast_rows(w_in_t, lambda j: nz + j, N_BRANCH * d // tc, ngate, tc, "cast_w_merge")
    gate_pad = (0, LANES - ngate)
    w_g = jnp.pad(w_in_t[:, OFF_GATES:OFF_MERGE, :], ((0, 0), gate_pad, (0, 0))).astype(BF16)
    b_z = jnp.concatenate([b_in[:, OFF_QKV:OFF_GATES], b_in[:, :OFF_QKV]], axis=-1)[:, None, :]
    b_m = b_in[:, None, OFF_MERGE:]
    b_g = jnp.pad(b_in[:, OFF_GATES:OFF_MERGE], ((0, 0), gate_pad))[:, None, :]
    w_a_out_b = w_a_out.astype(BF16)
    w_pool_b = w_pool.astype(BF16)
    w_c_out_b = w_c_out.astype(BF16)
    w_out_b = w_out.astype(BF16)
    w_up_b = w_ffn_up.astype(BF16)
    w_down_b = w_ffn_down.astype(BF16)

    x, h, zg = _entry(lay, x_prompt.reshape(lay.mp, d), x_sample.reshape(lay.ms, d), norm_mix_pre[0], mod, w_g, b_g)
    new_c = None
    new_n, new_m = [], []
    for l in range(depth):
        z = _matmul_bias(h, w_z, b_z, l, Z_WIDTH, 2048, 512, "in_proj")
        zg = zg[:, :4 * N_HEADS]

        conv_args = (conv_a_w[l], conv_a_b[l], ln_a_w[l], ln_a_b[l])
        a_act = _conv_a(z, 0, lay.mp, lp, *conv_args, None)
        a_act = _conv_a(z, lay.mp, lay.ms, GRID_W, *conv_args, a_act)
        p = _pool(z, 0, lay.mp, lp, 1, None)
        p = _pool(z, lay.mp, lay.ms, ls, GRID_W, p)
        hc, new_c, n_l, m_l = _mlstm(z, zg[:lay.mp], mlstm_norm_w[l], 0, bp, lp, None, l, None, (depth, new_c))
        (hc,) = _mlstm(z, zg[lay.mp:], mlstm_norm_w[l], lay.mp, bs, ls, (state_C, state_n, state_m), l, hc, None)
        new_n.append(n_l.reshape(bp, 2, N_HEADS, HEAD_DIM))
        new_m.append(m_l.reshape(bp, 2, N_HEADS))

        mix_in = _merge(h, a_act, p, hc, w_m, b_m, l, w_a_out_b, w_pool_b, pool_scale[l], w_c_out_b)
        x, h2 = _proj_residual(lay, mix_in, w_out_b, l, x, norm_mix_post[l], mod, (l, 2), 512, 256, "out_proj",
                               nxt=(norm_ffn_pre[l], (l, 4), (l, 3)))
        ffn_args = (w_up_b, l, ffn_conv_w[l], ffn_conv_b[l])
        tm_ffn = ls
        f_in = _ffn_up(h2, 0, lay.mp, tm_ffn, 1, lp, *ffn_args, None)
        f_in = _ffn_up(h2, lay.mp, lay.ms, tm_ffn, GRID_W, ls, *ffn_args, f_in)
        down_args = (lay, f_in, w_down_b, l, x, norm_ffn_post[l], mod, (l, 5), 256, 128, "ffn_down")
        if l + 1 < depth:
            x, h = _proj_residual(*down_args, nxt=(norm_mix_pre[l + 1], (l + 1, 1), (l + 1, 0)))
            zg = _matmul_bias(h, w_g, b_g, l + 1, LANES, 1024, LANES, "gate_proj")
        else:
            (y_prompt,) = _proj_residual(*down_args, row0=0, rows=lay.mp)
            (y_sample,) = _proj_residual(*down_args, row0=lay.mp, rows=lay.ms)
            y_prompt, y_sample = y_prompt.reshape(bp, lp, d), y_sample.reshape(bs, ls, d)
    return (y_prompt, y_sample, new_c, jnp.stack(new_n, axis=1), jnp.stack(new_m, axis=1))
```

```python
import functools

import jax
import jax.numpy as jnp
from jax import lax
from jax.experimental import pallas as pl
from jax.experimental.pallas import tpu as pltpu

GRID_W = 64
WA = 512
WB = 512
N_POOL_GROUPS = 4
POOL_GC = WB // N_POOL_GROUPS
POOL_WINDOWS = (2, 4, 8, 16)
WC = 1024
N_HEADS = 4
HEAD_DIM = WC // N_HEADS
CONV_A_WIDTH = 31
CHUNK = 64
N_BRANCH = 3
N_MOD = 6
EPS = 1e-6

OFF_POOL = 2 * WA
OFF_QKV = OFF_POOL + WB
OFF_OG = OFF_QKV + 3 * WC
OFF_GATES = OFF_OG + WC
OFF_MERGE = OFF_GATES + 4 * N_HEADS

Z_QKV = 0
Z_OG = Z_QKV + 3 * WC
Z_GLU = Z_OG + WC
Z_POOL = Z_GLU + 2 * WA
Z_WIDTH = Z_POOL + WB

LANES = 128
SUBLANES = 8
VMEM_LIMIT_BYTES = 56 * 1024 * 1024

N_COND_ROWS = 8
BF16 = jnp.bfloat16
F32 = jnp.float32


def _params(*sem):
    return pltpu.CompilerParams(dimension_semantics=sem, vmem_limit_bytes=VMEM_LIMIT_BYTES)


def _rms(x):
    return x * lax.rsqrt(jnp.mean(x * x, axis=-1, keepdims=True) + EPS)


def _ada_kernel(c_ref, w_ref, b_ref, o_ref):
    c = c_ref[...]
    a = c * jax.nn.sigmoid(c)
    o_ref[...] = jnp.dot(a, w_ref[...], preferred_element_type=F32) + b_ref[...]


def _ada(cond, w_ada, b_ada):
    depth, d, n = w_ada.shape
    tn = 1024
    return pl.pallas_call(
        _ada_kernel,
        grid=(depth, n // tn),
        in_specs=[pl.BlockSpec((N_COND_ROWS, d), lambda l, j: (0, 0)),
                  pl.BlockSpec((None, d, tn), lambda l, j: (l, 0, j)),
                  pl.BlockSpec((None, 1, tn), lambda l, j: (l, 0, j))],
        out_specs=pl.BlockSpec((None, N_COND_ROWS, tn), lambda l, j: (l, 0, j)),
        out_shape=jax.ShapeDtypeStruct((depth, N_COND_ROWS, n), F32),
        compiler_params=_params("arbitrary", "arbitrary"),
        name="ada",
    )(cond, w_ada, b_ada.reshape(depth, 1, n))


class _Layout:
    def __init__(self, bp, lp, bs, ls, d):
        self.bp, self.lp, self.bs, self.ls, self.d = bp, lp, bs, ls, d
        self.mp = bp * lp
        self.ms = bs * ls
        self.m = self.mp + self.ms

    def mod_spec(self, layer, k, tm):
        mp, ls, bs = self.mp, self.ls, self.bs

        def index(i, *_):
            start = i * tm
            row = jnp.where(start < mp, bs, (start - mp) // ls)
            return ((layer * N_COND_ROWS + row) * N_MOD + k, 0, 0)

        return pl.BlockSpec((None, 1, self.d), index)


def _gate_preact(h, wg_ref, bg_ref):
    return lax.dot_general(h, wg_ref[...], _NT, preferred_element_type=F32) + bg_ref[...]


def _entry_kernel(xp_ref, xs_ref, w_ref, sc_ref, sh_ref, wg_ref, bg_ref, xo_ref, ho_ref, zg_ref, *, n_ctx_tiles):
    x = jnp.where(pl.program_id(0) < n_ctx_tiles, xp_ref[...], xs_ref[...])
    xo_ref[...] = x
    h = ((_rms(x) * w_ref[...]) * (1.0 + sc_ref[...]) + sh_ref[...]).astype(ho_ref.dtype)
    ho_ref[...] = h
    zg_ref[...] = _gate_preact(h, wg_ref, bg_ref)


def _entry(lay, xp, xs, w, mod, w_g, b_g):
    tm = 512
    d = lay.d
    n_ctx = lay.mp // tm
    n_lat = lay.ms // tm
    tile = pl.BlockSpec((tm, d), lambda i: (i, 0))
    return pl.pallas_call(
        functools.partial(_entry_kernel, n_ctx_tiles=n_ctx),
        grid=(n_ctx + n_lat,),
        in_specs=[pl.BlockSpec((tm, d), lambda i: (jnp.minimum(i, n_ctx - 1), 0)),
                  pl.BlockSpec((tm, d), lambda i: (jnp.maximum(i - n_ctx, 0), 0)),
                  pl.BlockSpec((1, d), lambda i: (0, 0)),
                  lay.mod_spec(0, 1, tm), lay.mod_spec(0, 0, tm),
                  pl.BlockSpec((None, LANES, d), lambda i: (0, 0, 0)),
                  pl.BlockSpec((None, 1, LANES), lambda i: (0, 0, 0))],
        out_specs=[tile, tile, pl.BlockSpec((tm, LANES), lambda i: (i, 0))],
        out_shape=[jax.ShapeDtypeStruct((lay.m, d), F32), jax.ShapeDtypeStruct((lay.m, d), BF16),
                   jax.ShapeDtypeStruct((lay.m, LANES), F32)],
        compiler_params=_params("parallel"),
        name="entry",
    )(xp, xs, w.reshape(1, d), mod, mod, w_g, b_g)


_NT = (((1,), (1,)), ((), ()))


def _matmul_bias_kernel(a_ref, w_ref, b_ref, o_ref):
    o_ref[...] = lax.dot_general(a_ref[...], w_ref[...], _NT, preferred_element_type=F32) + b_ref[...]


def _matmul_bias(a, wt, b, layer, n, tm, tn, name):
    m, k = a.shape
    return pl.pallas_call(
        _matmul_bias_kernel,
        grid=(m // tm, n // tn),
        in_specs=[pl.BlockSpec((tm, k), lambda i, j: (i, 0)),
                  pl.BlockSpec((None, tn, k), lambda i, j: (layer, j, 0)),
                  pl.BlockSpec((None, 1, tn), lambda i, j: (layer, 0, j))],
        out_specs=pl.BlockSpec((tm, tn), lambda i, j: (i, j)),
        out_shape=jax.ShapeDtypeStruct((m, n), F32),
        compiler_params=_params("parallel", "arbitrary"),
        name=name,
    )(a, wt, b)


def _cast_rows_kernel(*refs, shift):
    o_ref = refs[-1]
    tn = o_ref.shape[0]
    if shift == 0:
        o_ref[...] = refs[0][...].astype(o_ref.dtype)
    else:
        o_ref[pl.ds(0, tn - shift), :] = refs[0][pl.ds(shift, tn - shift), :].astype(o_ref.dtype)
        o_ref[pl.ds(tn - shift, shift), :] = refs[1][pl.ds(0, shift), :].astype(o_ref.dtype)


def _cast_rows(wt, src_block, nblocks, shift, tn, name):
    depth, n_src, k = wt.shape
    in_specs = [pl.BlockSpec((None, tn, k), lambda l, j: (l, src_block(j), 0))]
    if shift:
        assert tn % shift == 0 and n_src % shift == 0
        in_specs.append(pl.BlockSpec((None, shift, k), lambda l, j: (l, (src_block(j) + 1) * (tn // shift), 0)))
    return pl.pallas_call(
        functools.partial(_cast_rows_kernel, shift=shift),
        grid=(depth, nblocks),
        in_specs=in_specs,
        out_specs=pl.BlockSpec((None, tn, k), lambda l, j: (l, j, 0)),
        out_shape=jax.ShapeDtypeStruct((depth, nblocks * tn, k), BF16),
        compiler_params=_params("parallel", "parallel"),
        name=name,
    )(*([wt] * len(in_specs)))


def _into(prev):
    if prev is None:
        return [], [], {}
    return [prev], [pl.BlockSpec(memory_space=pl.ANY)], {0: 0}


def _skip_refs(kernel_fn, n):
    if n == 0:
        return kernel_fn
    return lambda *refs: kernel_fn(*refs[n:])


CONV_PAD = 16
CONV_ROWS = 32


def _conv_a_kernel(z_ref, w_ref, b_ref, lw_ref, lb_ref, o_ref, pad_ref, shifted_ref, *, seg):
    t = z_ref.shape[0]
    stride = seg + 2 * CONV_PAD
    half = CONV_A_WIDTH // 2
    zeros = jnp.zeros((CONV_PAD, WA), F32)
    for s in range(t // seg):
        base = s * stride
        z = z_ref[pl.ds(s * seg, seg), :]
        pad_ref[pl.ds(base, CONV_PAD), :] = zeros
        pad_ref[pl.ds(base + CONV_PAD, seg), :] = z[:, :WA] * jax.nn.sigmoid(z[:, WA:])
        pad_ref[pl.ds(base + CONV_PAD + seg, CONV_PAD), :] = zeros
    nshift = pad_ref.shape[0] - SUBLANES
    for rho in range(1, SUBLANES):
        shifted_ref[rho - 1, pl.ds(0, nshift), :] = pad_ref[pl.ds(rho, nshift), :]
    for s in range(t // seg):
        for c in range(seg // CONV_ROWS):
            row0 = s * stride + CONV_PAD + c * CONV_ROWS
            acc = jnp.broadcast_to(b_ref[...], (CONV_ROWS, WA))
            for tap in range(CONV_A_WIDTH):
                rho = (tap - half) % SUBLANES
                rows = pl.ds(row0 + tap - half - rho, CONV_ROWS)
                window = pad_ref[rows, :] if rho == 0 else shifted_ref[rho - 1, rows, :]
                acc = acc + window * w_ref[pl.ds(tap, 1), :]
            mu = jnp.mean(acc, axis=-1, keepdims=True)
            cen = acc - mu
            var = jnp.mean(cen * cen, axis=-1, keepdims=True)
            y = cen * lax.rsqrt(var + EPS) * lw_ref[...] + lb_ref[...]
            y = y * jax.nn.sigmoid(y)
            o_ref[pl.ds(s * seg + c * CONV_ROWS, CONV_ROWS), :] = y.astype(o_ref.dtype)


def _conv_a(z, row0, rows, seg, conv_w, conv_b, ln_w, ln_b, prev):
    t = 256
    nseg = t // seg
    blk0 = row0 // t
    pre_args, pre_specs, aliases = _into(prev)
    return pl.pallas_call(
        _skip_refs(functools.partial(_conv_a_kernel, seg=seg), len(pre_args)),
        grid=(rows // t,),
        in_specs=pre_specs + [pl.BlockSpec((t, 2 * WA), lambda i: (blk0 + i, Z_GLU // (2 * WA))),
                              pl.BlockSpec((CONV_A_WIDTH, WA), lambda i: (0, 0)),
                              pl.BlockSpec((1, WA), lambda i: (0, 0)),
                              pl.BlockSpec((1, WA), lambda i: (0, 0)),
                              pl.BlockSpec((1, WA), lambda i: (0, 0))],
        out_specs=pl.BlockSpec((t, WA), lambda i: (blk0 + i, 0)),
        out_shape=jax.ShapeDtypeStruct((z.shape[0], WA), BF16),
        input_output_aliases=aliases,
        scratch_shapes=[pltpu.VMEM((nseg * (seg + 2 * CONV_PAD), WA), F32),
                        pltpu.VMEM((SUBLANES - 1, nseg * (seg + 2 * CONV_PAD), WA), F32)],
        compiler_params=_params("parallel"),
        name=f"conv_a_seg{seg}",
    )(*pre_args, z, conv_w, conv_b.reshape(1, WA), ln_w.reshape(1, WA), ln_b.reshape(1, WA))


POOL_ROWS = 64
POOL_HALF_MAX = max(POOL_WINDOWS) // 2


def _pool_kernel(z_ref, o_ref, pad_ref, *, dil):
    t = z_ref.shape[0]
    npos = t // dil
    padr = POOL_HALF_MAX * dil
    zeros = jnp.zeros((padr, WB), F32)
    pad_ref[pl.ds(0, padr), :] = zeros
    pad_ref[pl.ds(padr, t), :] = z_ref[...]
    pad_ref[pl.ds(padr + t, padr), :] = zeros
    for c in range(t // POOL_ROWS):
        r0 = c * POOL_ROWS
        pos = (r0 + lax.broadcasted_iota(jnp.int32, (POOL_ROWS, POOL_GC), 0)) >> (dil.bit_length() - 1)
        for g, win in enumerate(POOL_WINDOWS):
            lanes = pl.ds(g * POOL_GC, POOL_GC)
            acc = None
            for o in range(-(win // 2), win - win // 2):
                term = pad_ref[pl.ds(padr + r0 + o * dil, POOL_ROWS), lanes]
                acc = term if acc is None else acc + term
            lo = jnp.maximum(pos - win // 2, 0)
            hi = jnp.minimum(pos - win // 2 + win, npos)
            cnt = (hi - lo).astype(F32)
            x = pad_ref[pl.ds(padr + r0, POOL_ROWS), lanes]
            o_ref[pl.ds(r0, POOL_ROWS), lanes] = (acc / cnt - x).astype(o_ref.dtype)


def _pool(z, row0, rows, t, dil, prev):
    blk0 = row0 // t
    pre_args, pre_specs, aliases = _into(prev)
    return pl.pallas_call(
        _skip_refs(functools.partial(_pool_kernel, dil=dil), len(pre_args)),
        grid=(rows // t,),
        in_specs=pre_specs + [pl.BlockSpec((t, WB), lambda i: (blk0 + i, Z_POOL // WB))],
        out_specs=pl.BlockSpec((t, WB), lambda i: (blk0 + i, 0)),
        out_shape=jax.ShapeDtypeStruct((z.shape[0], WB), BF16),
        input_output_aliases=aliases,
        scratch_shapes=[pltpu.VMEM((t + 2 * POOL_HALF_MAX * dil, WB), F32)],
        compiler_params=_params("parallel"),
        name=f"pool_dil{dil}",
    )(*pre_args, z)


SCAN_CHUNK = 256
NSTREAM = 2 * N_HEADS
N_COLVEC = 4


def _log_sigmoid(x):
    return jnp.minimum(x, 0.0) - jnp.log1p(jnp.exp(-jnp.abs(x)))


def _lane_scan(x, op, ident, reverse):
    lane = lax.broadcasted_iota(jnp.int32, x.shape, 1)
    n = x.shape[1]
    sh = 1
    while sh < n:
        if reverse:
            x = op(x, jnp.where(lane < n - sh, pltpu.roll(x, n - sh, 1), ident))
        else:
            x = op(x, jnp.where(lane >= sh, pltpu.roll(x, sh, 1), ident))
        sh *= 2
    return x


def _mlstm_kernel(*refs, zero_init, emit_state):
    q_ref, k_ref, v_ref, og_ref, g_ref, nw_ref = refs[:6]
    pos = 6
    if not zero_init:
        c0_ref, n0_ref, m0_ref = refs[pos:pos + 3]
        pos += 3
    o_ref = refs[pos]
    pos += 1
    if emit_state:
        co_ref, no_ref, mo_ref = refs[pos:pos + 3]
        pos += 3
    rowbuf, decbuf, colbuf, h_scr = refs[pos:pos + 4]
    c_scr = refs[pos + 4:pos + 4 + NSTREAM]
    n_scr = refs[pos + 4 + NSTREAM:pos + 4 + 2 * NSTREAM]

    t = SCAN_CHUNK
    nsteps = q_ref.shape[0] // t

    log_i = g_ref[0]
    log_f = _log_sigmoid(g_ref[1])
    backward = (lax.broadcasted_iota(jnp.int32, log_i.shape, 0) & N_HEADS) != 0

    def scan(x, op, ident):
        return jnp.where(backward, _lane_scan(x, op, ident, True), _lane_scan(x, op, ident, False))

    b = scan(log_f, jnp.add, 0.0)
    c = log_i - b
    c_run = scan(c, jnp.maximum, -jnp.inf)
    b_tot = jnp.sum(log_f, axis=1, keepdims=True)
    c_tot = jnp.max(c, axis=1, keepdims=True)
    m = jnp.zeros((NSTREAM, 1), F32) if zero_init else m0_ref[...]
    unused_rows = jnp.zeros((LANES - N_COLVEC * NSTREAM, t), F32)
    for i in range(nsteps):
        sl = slice(i * NSTREAM, (i + 1) * NSTREAM)
        m_last = jnp.maximum(m, c_tot[sl])
        m_run = jnp.maximum(m, c_run[sl])
        packed = [m_run, jnp.exp(m - m_run), jnp.exp(-(b[sl] + m_run)), jnp.exp(c[sl] - m_last), unused_rows]
        colbuf[i] = jnp.concatenate(packed, axis=0).T
        rowbuf[i] = c[sl]
        decbuf[i] = jnp.broadcast_to(jnp.exp(m - m_last), (NSTREAM, HEAD_DIM))
        m = b_tot[sl] + m_last
    if emit_state:
        mo_ref[...] = m

    if not zero_init:
        for s in range(NSTREAM):
            d, hd = divmod(s, N_HEADS)
            c_scr[s][...] = c0_ref[d, hd]
            n_scr[s][...] = n0_ref[s:s + 1, :]

    t_idx = lax.broadcasted_iota(jnp.int32, (t, t), 0)
    s_idx = lax.broadcasted_iota(jnp.int32, (t, t), 1)
    scanned = (s_idx <= t_idx, s_idx >= t_idx)

    def step(i, has_state, update_state):
        col = colbuf[i]

        def colvec(vec, s):
            return col[:, vec * NSTREAM + s:vec * NSTREAM + s + 1]

        def where(s):
            d, hd = divmod(s, N_HEADS)
            cidx = i if d == 0 else nsteps - 1 - i
            return d, pl.ds(pl.multiple_of(cidx * t, t), t), pl.ds(hd * HEAD_DIM, HEAD_DIM)

        scores = []
        for s in range(NSTREAM):
            d, rows, cols = where(s)
            dmat = jnp.exp(jnp.where(scanned[d], rowbuf[i, s:s + 1, :] - colvec(0, s), -jnp.inf))
            q = q_ref[rows, cols] * (HEAD_DIM ** -0.5)
            scores.append(lax.dot_general(q, k_ref[rows, cols], (((1,), (1,)), ((), ())),
                                          preferred_element_type=F32) * dmat)
        for s in range(NSTREAM):
            d, rows, cols = where(s)
            num = jnp.dot(scores[s], v_ref[rows, cols], preferred_element_type=F32)
            den = jnp.sum(scores[s], axis=1, keepdims=True)
            if has_state:
                q = q_ref[rows, cols] * (HEAD_DIM ** -0.5)
                w_int = colvec(1, s)
                num = num + w_int * jnp.dot(q, c_scr[s][...], preferred_element_type=F32)
                den = den + w_int * jnp.sum(q * n_scr[s][...], axis=1, keepdims=True)
            h_scr[d, rows, cols] = num * (1.0 / jnp.maximum(jnp.abs(den), colvec(2, s)))
        if update_state:
            for s in range(NSTREAM):
                d, rows, cols = where(s)
                wk = k_ref[rows, cols] * colvec(3, s)
                c_new = lax.dot_general(wk, v_ref[rows, cols], (((0,), (0,)), ((), ())),
                                        preferred_element_type=F32)
                n_new = jnp.sum(wk, axis=0, keepdims=True)
                if has_state:
                    dec = decbuf[i, s:s + 1, :]
                    c_new = c_new + dec * c_scr[s][...]
                    n_new = n_new + dec * n_scr[s][...]
                c_scr[s][...] = c_new
                n_scr[s][...] = n_new

    first = 1 if zero_init else 0
    last = nsteps if emit_state else nsteps - 1
    if zero_init:
        step(0, False, emit_state or nsteps > 1)
    if last - first == 1:
        step(first, True, True)
    elif last > first:
        def body(i, carry):
            step(i, True, True)
            return carry
        lax.fori_loop(first, last, body, 0)
    if not emit_state and not (zero_init and nsteps == 1):
        step(nsteps - 1, True, False)

    for hd in range(N_HEADS):
        cols = pl.ds(hd * HEAD_DIM, HEAD_DIM)
        hn = _rms(h_scr[0, :, cols] + h_scr[1, :, cols])
        o_ref[:, cols] = (hn * nw_ref[:, cols] * jax.nn.sigmoid(og_ref[:, cols])).astype(o_ref.dtype)
    if emit_state:
        for s in range(NSTREAM):
            d, hd = divmod(s, N_HEADS)
            co_ref[d, hd] = c_scr[s][...]
            no_ref[s:s + 1, :] = n_scr[s][...]


def _mlstm(z, gates, norm_w, row0, nb, seq, init_state, layer, prev_hc, state_out):
    m_tot = z.shape[0]
    t = SCAN_CHUNK
    nsteps = seq // t
    blk0 = row0 // seq
    nrow = nsteps * NSTREAM
    g = gates.reshape(nb, nsteps, t, 2, 2, N_HEADS).transpose(0, 4, 3, 1, 5, 2)
    g = jnp.concatenate([g[:, :, 0], g[:, :, 1, ::-1]], axis=3)
    g = g.reshape(nb, 2, nrow, t)

    big = seq * WC * 4 > (2 << 20)

    def zspec(off):
        idx = lambda b: (blk0 + b, off // WC)
        return pl.BlockSpec((seq, WC), idx, pipeline_mode=pl.Buffered(1)) if big else pl.BlockSpec((seq, WC), idx)

    in_specs = [zspec(Z_QKV), zspec(Z_QKV + WC), zspec(Z_QKV + 2 * WC), zspec(Z_OG),
                pl.BlockSpec((None, 2, nrow, t), lambda b: (b, 0, 0, 0)),
                pl.BlockSpec((1, WC), lambda b: (0, 0))]
    args = [z, z, z, z, g, norm_w.reshape(1, WC)]
    if init_state is not None:
        sc, sn, sm = init_state
        depth = sc.shape[1]
        in_specs += [pl.BlockSpec((None, None, 2, N_HEADS, HEAD_DIM, HEAD_DIM), lambda b: (b, layer, 0, 0, 0, 0)),
                     pl.BlockSpec((None, None, NSTREAM, HEAD_DIM), lambda b: (b, layer, 0, 0)),
                     pl.BlockSpec((None, None, NSTREAM, 1), lambda b: (b, layer, 0, 0))]
        args += [sc, sn.reshape(nb, depth, NSTREAM, HEAD_DIM), sm.reshape(nb, depth, NSTREAM, 1)]
    out_specs = [pl.BlockSpec((seq, WC), lambda b: (blk0 + b, 0))]
    out_shape = [jax.ShapeDtypeStruct((m_tot, WC), BF16)]
    aliases = {}
    pre_args, pre_specs = [], []
    if prev_hc is not None:
        pre_args.append(prev_hc)
        pre_specs.append(pl.BlockSpec(memory_space=pl.ANY))
        aliases[0] = 0
    if state_out is not None:
        depth, prev_c = state_out
        out_specs += [pl.BlockSpec((None, None, 2, N_HEADS, HEAD_DIM, HEAD_DIM), lambda b: (b, layer, 0, 0, 0, 0)),
                      pl.BlockSpec((None, NSTREAM, HEAD_DIM), lambda b: (b, 0, 0)),
                      pl.BlockSpec((None, NSTREAM, 1), lambda b: (b, 0, 0))]
        out_shape += [jax.ShapeDtypeStruct((nb, depth, 2, N_HEADS, HEAD_DIM, HEAD_DIM), F32),
                      jax.ShapeDtypeStruct((nb, NSTREAM, HEAD_DIM), F32),
                      jax.ShapeDtypeStruct((nb, NSTREAM, 1), F32)]
        if prev_c is not None:
            aliases[len(pre_args)] = 1
            pre_args.append(prev_c)
            pre_specs.append(pl.BlockSpec(memory_space=pl.ANY))
    body = functools.partial(_mlstm_kernel, zero_init=init_state is None, emit_state=state_out is not None)
    return pl.pallas_call(
        _skip_refs(body, len(pre_args)),
        grid=(nb,),
        in_specs=pre_specs + in_specs,
        out_specs=out_specs,
        out_shape=out_shape,
        input_output_aliases=aliases,
        scratch_shapes=[pltpu.VMEM((nsteps, NSTREAM, t), F32),
                        pltpu.VMEM((nsteps, NSTREAM, HEAD_DIM), F32),
                        pltpu.VMEM((nsteps, t, LANES), F32),
                        pltpu.VMEM((2, seq, WC), F32)]
        + [pltpu.VMEM((HEAD_DIM, HEAD_DIM), F32)] * NSTREAM + [pltpu.VMEM((1, HEAD_DIM), F32)] * NSTREAM,
        compiler_params=_params("parallel"),
        name=f"mlstm_seq{seq}",
    )(*pre_args, *args)


def _merge_kernel(h_ref, a_ref, p_ref, c_ref, wga_ref, wgb_ref, wgc_ref, bga_ref, bgb_ref, bgc_ref,
                  wa_ref, wp_ref, ps_ref, wc_ref, o_ref):
    h = h_ref[...]

    def gate(w_ref, b_ref):
        return jax.nn.sigmoid(lax.dot_general(h, w_ref[...], _NT, preferred_element_type=F32) + b_ref[...])

    ya = jnp.dot(a_ref[...], wa_ref[...], preferred_element_type=F32)
    yb = jnp.dot(p_ref[...], wp_ref[...], preferred_element_type=F32) * ps_ref[...]
    yc = jnp.dot(c_ref[...], wc_ref[...], preferred_element_type=F32)
    mix = gate(wga_ref, bga_ref) * ya + gate(wgb_ref, bgb_ref) * yb + gate(wgc_ref, bgc_ref) * yc
    o_ref[...] = mix.astype(o_ref.dtype)


def _merge(h, a_act, p, hc, w_gates_t, b_gates, layer, w_a_out, w_pool, pool_scale, w_c_out):
    m, d = h.shape
    tm = 1024
    tn = d // N_POOL_GROUPS
    nj = d // tn

    def wg(k):
        return pl.BlockSpec((None, tn, d), lambda i, j: (layer, k * nj + j, 0))

    def bg(k):
        return pl.BlockSpec((None, 1, tn), lambda i, j: (layer, 0, k * nj + j))

    return pl.pallas_call(
        _merge_kernel,
        grid=(m // tm, nj),
        in_specs=[pl.BlockSpec((tm, d), lambda i, j: (i, 0)),
                  pl.BlockSpec((tm, WA), lambda i, j: (i, 0)),
                  pl.BlockSpec((tm, POOL_GC), lambda i, j: (i, j)),
                  pl.BlockSpec((tm, WC), lambda i, j: (i, 0)),
                  wg(0), wg(1), wg(2), bg(0), bg(1), bg(2),
                  pl.BlockSpec((None, WA, tn), lambda i, j: (layer, 0, j)),
                  pl.BlockSpec((None, None, POOL_GC, tn), lambda i, j: (layer, j, 0, 0)),
                  pl.BlockSpec((1, tn), lambda i, j: (0, j)),
                  pl.BlockSpec((None, WC, tn), lambda i, j: (layer, 0, j))],
        out_specs=pl.BlockSpec((tm, tn), lambda i, j: (i, j)),
        out_shape=jax.ShapeDtypeStruct((m, d), BF16),
        compiler_params=_params("parallel", "arbitrary"),
        name="merge",
    )(h, a_act, p, hc, w_gates_t, w_gates_t, w_gates_t, b_gates, b_gates, b_gates,
      w_a_out, w_pool, pool_scale.reshape(1, d), w_c_out)


def _proj_residual_kernel(*refs, sub, emit_next):
    a_ref, w_ref, x_ref, wpost_ref, gate_ref = refs[:5]
    if emit_next:
        wpre_ref, sc_ref, sh_ref, xo_ref, ho_ref = refs[5:]
    else:
        (xo_ref,) = refs[5:]
    for r in range(a_ref.shape[0] // sub):
        rows = pl.ds(r * sub, sub)
        y = jnp.dot(a_ref[rows, :], w_ref[...], preferred_element_type=F32)
        xn = x_ref[rows, :] + gate_ref[...] * (_rms(y) * wpost_ref[...])
        xo_ref[rows, :] = xn
        if emit_next:
            hn = _rms(xn) * wpre_ref[...]
            ho_ref[rows, :] = (hn * (1.0 + sc_ref[...]) + sh_ref[...]).astype(ho_ref.dtype)


def _proj_residual(lay, a, w, layer, x, w_post, mod, gate, tm, sub, name, nxt=None, row0=0, rows=None):
    d = lay.d
    kdim = a.shape[1]
    rows = lay.m if rows is None else rows
    blk0 = row0 // tm
    row = pl.BlockSpec((1, d), lambda i: (0, 0))
    tile = lambda width: pl.BlockSpec((tm, width), lambda i: (blk0 + i, 0))

    def mod_at(layer_k, k):
        spec = lay.mod_spec(layer_k, k, tm)
        return pl.BlockSpec(spec.block_shape, lambda i: spec.index_map(blk0 + i))

    in_specs = [tile(kdim),
                pl.BlockSpec((None, kdim, d), lambda i: (layer, 0, 0), pipeline_mode=pl.Buffered(1)),
                tile(d), row, mod_at(*gate)]
    args = [a, w, x, w_post.reshape(1, d), mod]
    out_tile = pl.BlockSpec((tm, d), lambda i: (i, 0))
    out_specs, out_shape = [out_tile], [jax.ShapeDtypeStruct((rows, d), F32)]
    if nxt is not None:
        assert row0 == 0 and rows == lay.m
        w_pre_next, scale, shift = nxt
        in_specs += [row, mod_at(*scale), mod_at(*shift)]
        args += [w_pre_next.reshape(1, d), mod, mod]
        out_specs.append(out_tile)
        out_shape.append(jax.ShapeDtypeStruct((rows, d), BF16))
    body = functools.partial(_proj_residual_kernel, sub=sub, emit_next=nxt is not None)
    return pl.pallas_call(
        body,
        grid=(rows // tm,),
        in_specs=in_specs,
        out_specs=out_specs,
        out_shape=out_shape,
        compiler_params=_params("parallel"),
        name=name,
    )(*args)


def _gelu_tanh(x):
    return 0.5 * x * (1.0 + jnp.tanh(0.7978845608028654 * (x + 0.044715 * (x * x * x))))


MXU_WIDTH = 256


def _ffn_up_kernel(h_ref, wu_ref, wg_ref, cw_ref, cb_ref, o_ref, *, shift, seg):
    h = h_ref[...]
    tm = h.shape[0]
    blocks = [pl.ds(c * MXU_WIDTH, MXU_WIDTH) for c in range(o_ref.shape[1] // MXU_WIDTH)]

    def activation(cols, g):
        if shift % SUBLANES == 0 and seg == tm:
            zeros = jnp.zeros((shift, MXU_WIDTH), F32)
            prev = jnp.concatenate([zeros, g[:tm - shift]], axis=0)
            nxt = jnp.concatenate([g[shift:], zeros], axis=0)
        else:
            r = lax.broadcasted_iota(jnp.int32, g.shape, 0) & (seg - 1)
            prev = jnp.where(r < shift, 0.0, pltpu.roll(g, shift, 0))
            nxt = jnp.where(r >= seg - shift, 0.0, pltpu.roll(g, tm - shift, 0))
        gc = cw_ref[0:1, cols] * prev + cw_ref[1:2, cols] * g + cw_ref[2:3, cols] * nxt + cb_ref[:, cols]
        return _gelu_tanh(gc)

    gates = [jnp.dot(h, wg_ref[:, cols], preferred_element_type=F32) for cols in blocks]
    acts = [activation(cols, g) for cols, g in zip(blocks, gates)]
    for cols, act in zip(blocks, acts):
        u = jnp.dot(h, wu_ref[:, cols], preferred_element_type=F32)
        o_ref[:, cols] = (act * u).astype(o_ref.dtype)


def _ffn_up(h2, row0, rows, tm, shift, seg, w_up, layer, conv_w, conv_b, prev):
    m, d = h2.shape
    dff = w_up.shape[2] // 2
    tn = 512
    nj = dff // tn
    blk0 = row0 // tm
    pre_args, pre_specs, aliases = _into(prev)
    return pl.pallas_call(
        _skip_refs(functools.partial(_ffn_up_kernel, shift=shift, seg=seg), len(pre_args)),
        grid=(rows // tm, nj),
        in_specs=pre_specs + [pl.BlockSpec((tm, d), lambda i, j: (blk0 + i, 0)),
                              pl.BlockSpec((None, d, tn), lambda i, j: (layer, 0, j)),
                              pl.BlockSpec((None, d, tn), lambda i, j: (layer, 0, nj + j)),
                              pl.BlockSpec((3, tn), lambda i, j: (0, j)),
                              pl.BlockSpec((1, tn), lambda i, j: (0, j))],
        out_specs=pl.BlockSpec((tm, tn), lambda i, j: (blk0 + i, j)),
        out_shape=jax.ShapeDtypeStruct((m, dff), BF16),
        input_output_aliases=aliases,
        compiler_params=_params("parallel", "arbitrary"),
        name=f"ffn_up_shift{shift}",
    )(*pre_args, h2, w_up, w_up, conv_w, conv_b.reshape(1, dff))


def kernel(x_prompt, x_sample, state_C, state_n, state_m, c, c_ctx, w_ada, b_ada, norm_mix_pre, norm_mix_post, norm_ffn_pre, norm_ffn_post, w_in, b_in, conv_a_w, conv_a_b, ln_a_w, ln_a_b, w_a_out, w_pool, pool_scale, mlstm_norm_w, w_c_out, w_out, w_ffn_up, ffn_conv_w, ffn_conv_b, w_ffn_down):
    bp, lp, d = x_prompt.shape
    bs, ls, _ = x_sample.shape
    depth = w_ada.shape[0]
    lay = _Layout(bp, lp, bs, ls, d)
    assert bs < N_COND_ROWS and lay.mp % (2 * ls) == 0 and lay.ms % (2 * ls) == 0 and ls % lp == 0
    assert lp & (lp - 1) == 0 and GRID_W & (GRID_W - 1) == 0
    assert lp % SCAN_CHUNK == 0 and ls % SCAN_CHUNK == 0 and N_COLVEC * NSTREAM <= LANES

    cond = jnp.zeros((N_COND_ROWS, d), F32).at[:bs].set(c).at[bs].set(c_ctx)
    mod = _ada(cond, w_ada, b_ada).reshape(depth * N_COND_ROWS * N_MOD, 1, d)

    tc = 512
    ngate = OFF_MERGE - OFF_GATES
    assert OFF_QKV % tc == 0 and OFF_GATES % tc == 0 and ngate % 16 == 0 and ngate < tc
    nz = OFF_GATES // tc
    w_in_t = jnp.swapaxes(w_in, 1, 2)
    w_z = _cast_rows(w_in_t, lambda j: (j + OFF_QKV // tc) % nz, nz, 0, tc, "cast_w_z")
    w_m =_cast_rows(w_in_t, lambda j: nz + j, N_BRANCH * d // tc, ngate, tc, "cast_w_merge")
    gate_pad = (0, LANES - ngate)
    w_g = jnp.pad(w_in_t[:, OFF_GATES:OFF_MERGE, :], ((0, 0), gate_pad, (0, 0))).astype(BF16)
    b_z = jnp.concatenate([b_in[:, OFF_QKV:OFF_GATES], b_in[:, :OFF_QKV]], axis=-1)[:, None, :]
    b_m = b_in[:, None, OFF_MERGE:]
    b_g = jnp.pad(b_in[:, OFF_GATES:OFF_MERGE], ((0, 0), gate_pad))[:, None, :]
    w_a_out_b = w_a_out.astype(BF16)
    w_pool_b = w_pool.astype(BF16)
    w_c_out_b = w_c_out.astype(BF16)
    w_out_b = w_out.astype(BF16)
    w_up_b = w_ffn_up.astype(BF16)
    w_down_b = w_ffn_down.astype(BF16)

    x, h, zg = _entry(lay, x_prompt.reshape(lay.mp, d), x_sample.reshape(lay.ms, d), norm_mix_pre[0], mod, w_g, b_g)
    new_c = None
    new_n, new_m = [], []
    for l in range(depth):
        z = _matmul_bias(h, w_z, b_z, l, Z_WIDTH, 2048, 512, "in_proj")
        zg = zg[:, :4 * N_HEADS]

        conv_args = (conv_a_w[l], conv_a_b[l], ln_a_w[l], ln_a_b[l])
        a_act = _conv_a(z, 0, lay.mp, lp, *conv_args, None)
        a_act = _conv_a(z, lay.mp, lay.ms, GRID_W, *conv_args, a_act)
        p = _pool(z, 0, lay.mp, lp, 1, None)
        p = _pool(z, lay.mp, lay.ms, ls, GRID_W, p)
        hc, new_c, n_l, m_l = _mlstm(z, zg[:lay.mp], mlstm_norm_w[l], 0, bp, lp, None, l, None, (depth, new_c))
        (hc,) = _mlstm(z, zg[lay.mp:], mlstm_norm_w[l], lay.mp, bs, ls, (state_C, state_n, state_m), l, hc, None)
        new_n.append(n_l.reshape(bp, 2, N_HEADS, HEAD_DIM))
        new_m.append(m_l.reshape(bp, 2, N_HEADS))

        mix_in = _merge(h, a_act, p, hc, w_m, b_m, l, w_a_out_b, w_pool_b, pool_scale[l], w_c_out_b)
        x, h2 = _proj_residual(lay, mix_in, w_out_b, l, x, norm_mix_post[l], mod, (l, 2), 512, 256, "out_proj",
                               nxt=(norm_ffn_pre[l], (l, 4), (l, 3)))
        ffn_args = (w_up_b, l, ffn_conv_w[l], ffn_conv_b[l])
        tm_ffn = ls
        f_in = _ffn_up(h2, 0, lay.mp, tm_ffn, 1, lp, *ffn_args, None)
        f_in = _ffn_up(h2, lay.mp, lay.ms, tm_ffn, GRID_W, ls, *ffn_args, f_in)
        down_args = (lay, f_in, w_down_b, l, x, norm_ffn_post[l], mod, (l, 5), 256, 128, "ffn_down")
        if l + 1 < depth:
            x, h = _proj_residual(*down_args, nxt=(norm_mix_pre[l + 1], (l + 1, 1), (l + 1, 0)))
            zg = _matmul_bias(h, w_g, b_g, l + 1, LANES, 1024, LANES, "gate_proj")
        else:
            (y_prompt,) = _proj_residual(*down_args, row0=0, rows=lay.mp)
            (y_sample,) = _proj_residual(*down_args, row0=lay.mp, rows=lay.ms)
            y_prompt, y_sample = y_prompt.reshape(bp, lp, d), y_sample.reshape(bs, ls, d)
    return (y_prompt, y_sample, new_c, jnp.stack(new_n, axis=1), jnp.stack(new_m, axis=1))
```

```python
import functools

import jax
import jax.numpy as jnp
from jax import lax
from jax.experimental import pallas as pl
from jax.experimental.pallas import tpu as pltpu

GRID_W = 64
WA = 512
WB = 512
N_POOL_GROUPS = 4
POOL_GC = WB // N_POOL_GROUPS
POOL_WINDOWS = (2, 4, 8, 16)
WC = 1024
N_HEADS = 4
HEAD_DIM = WC // N_HEADS
CONV_A_WIDTH = 31
CHUNK = 64
N_BRANCH = 3
N_MOD = 6
EPS = 1e-6

OFF_POOL = 2 * WA
OFF_QKV = OFF_POOL + WB
OFF_OG = OFF_QKV + 3 * WC
OFF_GATES = OFF_OG + WC
OFF_MERGE = OFF_GATES + 4 * N_HEADS

Z_QKV = 0
Z_OG = Z_QKV + 3 * WC
Z_GLU = Z_OG + WC
Z_POOL = Z_GLU + 2 * WA
Z_WIDTH = Z_POOL + WB

LANES = 128
SUBLANES = 8
VMEM_LIMIT_BYTES = 56 * 1024 * 1024

N_COND_ROWS = 8
BF16 = jnp.bfloat16
F32 = jnp.float32


def _params(*sem):
    return pltpu.CompilerParams(dimension_semantics=sem, vmem_limit_bytes=VMEM_LIMIT_BYTES)


def _rms(x):
    return x * lax.rsqrt(jnp.mean(x * x, axis=-1, keepdims=True) + EPS)


def _ada_kernel(c_ref, w_ref, b_ref, o_ref):
    c = c_ref[...]
    a = c * jax.nn.sigmoid(c)
    o_ref[...] = jnp.dot(a, w_ref[...], preferred_element_type=F32) + b_ref[...]


def _ada(cond, w_ada, b_ada):
    depth, d, n = w_ada.shape
    tn = 1024
    return pl.pallas_call(
        _ada_kernel,
        grid=(depth, n // tn),
        in_specs=[pl.BlockSpec((N_COND_ROWS, d), lambda l, j: (0, 0)),
                  pl.BlockSpec((None, d, tn), lambda l, j: (l, 0, j)),
                  pl.BlockSpec((None, 1, tn), lambda l, j: (l, 0, j))],
        out_specs=pl.BlockSpec((None, N_COND_ROWS, tn), lambda l, j: (l, 0, j)),
        out_shape=jax.ShapeDtypeStruct((depth, N_COND_ROWS, n), F32),
        compiler_params=_params("arbitrary", "arbitrary"),
        name="ada",
    )(cond, w_ada, b_ada.reshape(depth, 1, n))


class _Layout:
    def __init__(self, bp, lp, bs, ls, d):
        self.bp, self.lp, self.bs, self.ls, self.d = bp, lp, bs, ls, d
        self.mp = bp * lp
        self.ms = bs * ls
        self.m = self.mp + self.ms

    def mod_spec(self, layer, k, tm):
        mp, ls, bs = self.mp, self.ls, self.bs

        def index(i, *_):
            start = i * tm
            row = jnp.where(start < mp, bs, (start - mp) // ls)
            return ((layer * N_COND_ROWS + row) * N_MOD + k, 0, 0)

        return pl.BlockSpec((None, 1, self.d), index)


def _gate_preact(h, wg_ref, bg_ref):
    return lax.dot_general(h, wg_ref[...], _NT, preferred_element_type=F32) + bg_ref[...]


def _entry_kernel(xp_ref, xs_ref, w_ref, sc_ref, sh_ref, wg_ref, bg_ref, xo_ref, ho_ref, zg_ref, *, n_ctx_tiles):
    x = jnp.where(pl.program_id(0) < n_ctx_tiles, xp_ref[...], xs_ref[...])
    xo_ref[...] = x
    h = ((_rms(x) * w_ref[...]) * (1.0 + sc_ref[...]) + sh_ref[...]).astype(ho_ref.dtype)
    ho_ref[...] = h
    zg_ref[...] = _gate_preact(h, wg_ref, bg_ref)


def _entry(lay, xp, xs, w, mod, w_g, b_g):
    tm = 512
    d = lay.d
    n_ctx = lay.mp // tm
    n_lat = lay.ms // tm
    tile = pl.BlockSpec((tm, d), lambda i: (i, 0))
    return pl.pallas_call(
        functools.partial(_entry_kernel, n_ctx_tiles=n_ctx),
        grid=(n_ctx + n_lat,),
        in_specs=[pl.BlockSpec((tm, d), lambda i: (jnp.minimum(i, n_ctx - 1), 0)),
                  pl.BlockSpec((tm, d), lambda i: (jnp.maximum(i - n_ctx, 0), 0)),
                  pl.BlockSpec((1, d), lambda i: (0, 0)),
                  lay.mod_spec(0, 1, tm), lay.mod_spec(0, 0, tm),
                  pl.BlockSpec((None, LANES, d), lambda i: (0, 0, 0)),
                  pl.BlockSpec((None, 1, LANES), lambda i: (0, 0, 0))],
        out_specs=[tile, tile, pl.BlockSpec((tm, LANES), lambda i: (i, 0))],
        out_shape=[jax.ShapeDtypeStruct((lay.m, d), F32), jax.ShapeDtypeStruct((lay.m, d), BF16),
                   jax.ShapeDtypeStruct((lay.m, LANES), F32)],
        compiler_params=_params("parallel"),
        name="entry",
    )(xp, xs, w.reshape(1, d), mod, mod, w_g, b_g)


_NT = (((1,), (1,)), ((), ()))


def _matmul_bias_kernel(a_ref, w_ref, b_ref, o_ref):
    o_ref[...] = lax.dot_general(a_ref[...], w_ref[...], _NT, preferred_element_type=F32) + b_ref[...]


def _matmul_bias(a, wt, b, layer, n, tm, tn, name):
    m, k = a.shape
    return pl.pallas_call(
        _matmul_bias_kernel,
        grid=(m // tm, n // tn),
        in_specs=[pl.BlockSpec((tm, k), lambda i, j: (i, 0)),
                  pl.BlockSpec((None, tn, k), lambda i, j: (layer, j, 0)),
                  pl.BlockSpec((None, 1, tn), lambda i, j: (layer, 0, j))],
        out_specs=pl.BlockSpec((tm, tn), lambda i, j: (i, j)),
        out_shape=jax.ShapeDtypeStruct((m, n), F32),
        compiler_params=_params("parallel", "arbitrary"),
        name=name,
    )(a, wt, b)


def _cast_rows_kernel(*refs, shift):
    o_ref = refs[-1]
    tn = o_ref.shape[0]
    if shift == 0:
        o_ref[...] = refs[0][...].astype(o_ref.dtype)
    else:
        o_ref[pl.ds(0, tn - shift), :] = refs[0][pl.ds(shift, tn - shift), :].astype(o_ref.dtype)
        o_ref[pl.ds(tn - shift, shift), :] = refs[1][pl.ds(0, shift), :].astype(o_ref.dtype)


def _cast_rows(wt, src_block, nblocks, shift, tn, name):
    depth, n_src, k = wt.shape
    in_specs = [pl.BlockSpec((None, tn, k), lambda l, j: (l, src_block(j), 0))]
    if shift:
        assert tn % shift == 0 and n_src % shift == 0
        in_specs.append(pl.BlockSpec((None, shift, k), lambda l, j: (l, (src_block(j) + 1) * (tn // shift), 0)))
    return pl.pallas_call(
        functools.partial(_cast_rows_kernel, shift=shift),
        grid=(depth, nblocks),
        in_specs=in_specs,
        out_specs=pl.BlockSpec((None, tn, k), lambda l, j: (l, j, 0)),
        out_shape=jax.ShapeDtypeStruct((depth, nblocks * tn, k), BF16),
        compiler_params=_params("parallel", "parallel"),
        name=name,
    )(*([wt] * len(in_specs)))


def _into(prev):
    if prev is None:
        return [], [], {}
    return [prev], [pl.BlockSpec(memory_space=pl.ANY)], {0: 0}


def _skip_refs(kernel_fn, n):
    if n == 0:
        return kernel_fn
    return lambda *refs: kernel_fn(*refs[n:])


CONV_PAD = 16
CONV_ROWS = 32


def _conv_a_kernel(z_ref, w_ref, b_ref, lw_ref, lb_ref, o_ref, pad_ref, shifted_ref, *, seg):
    t = z_ref.shape[0]
    stride = seg + 2 * CONV_PAD
    half = CONV_A_WIDTH // 2
    zeros = jnp.zeros((CONV_PAD, WA), F32)
    for s in range(t // seg):
        base = s * stride
        z = z_ref[pl.ds(s * seg, seg), :]
        pad_ref[pl.ds(base, CONV_PAD), :] = zeros
        pad_ref[pl.ds(base + CONV_PAD, seg), :] = z[:, :WA] * jax.nn.sigmoid(z[:, WA:])
        pad_ref[pl.ds(base + CONV_PAD + seg, CONV_PAD), :] = zeros
    nshift = pad_ref.shape[0] - SUBLANES
    for rho in range(1, SUBLANES):
        shifted_ref[rho - 1, pl.ds(0, nshift), :] = pad_ref[pl.ds(rho, nshift), :]
    for s in range(t // seg):
        for c in range(seg // CONV_ROWS):
            row0 = s * stride + CONV_PAD + c * CONV_ROWS
            acc = jnp.broadcast_to(b_ref[...], (CONV_ROWS, WA))
            for tap in range(CONV_A_WIDTH):
                rho = (tap - half) % SUBLANES
                rows = pl.ds(row0 + tap - half - rho, CONV_ROWS)
                window = pad_ref[rows, :] if rho == 0 else shifted_ref[rho - 1, rows, :]
                acc = acc + window * w_ref[pl.ds(tap, 1), :]
            mu = jnp.mean(acc, axis=-1, keepdims=True)
            cen = acc - mu
            var = jnp.mean(cen * cen, axis=-1, keepdims=True)
            y = cen * lax.rsqrt(var + EPS) * lw_ref[...] + lb_ref[...]
            y = y * jax.nn.sigmoid(y)
            o_ref[pl.ds(s * seg + c * CONV_ROWS, CONV_ROWS), :] = y.astype(o_ref.dtype)


def _conv_a(z, row0, rows, seg, conv_w, conv_b, ln_w, ln_b, prev):
    t = 256
    nseg = t // seg
    blk0 = row0 // t
    pre_args, pre_specs, aliases = _into(prev)
    return pl.pallas_call(
        _skip_refs(functools.partial(_conv_a_kernel, seg=seg), len(pre_args)),
        grid=(rows // t,),
        in_specs=pre_specs + [pl.BlockSpec((t, 2 * WA), lambda i: (blk0 + i, Z_GLU // (2 * WA))),
                              pl.BlockSpec((CONV_A_WIDTH, WA), lambda i: (0, 0)),
                              pl.BlockSpec((1, WA), lambda i: (0, 0)),
                              pl.BlockSpec((1, WA), lambda i: (0, 0)),
                              pl.BlockSpec((1, WA), lambda i: (0, 0))],
        out_specs=pl.BlockSpec((t, WA), lambda i: (blk0 + i, 0)),
        out_shape=jax.ShapeDtypeStruct((z.shape[0], WA), BF16),
        input_output_aliases=aliases,
        scratch_shapes=[pltpu.VMEM((nseg * (seg + 2 * CONV_PAD), WA), F32),
                        pltpu.VMEM((SUBLANES - 1, nseg * (seg + 2 * CONV_PAD), WA), F32)],
        compiler_params=_params("parallel"),
        name=f"conv_a_seg{seg}",
    )(*pre_args, z, conv_w, conv_b.reshape(1, WA), ln_w.reshape(1, WA), ln_b.reshape(1, WA))


POOL_ROWS = 64
POOL_HALF_MAX = max(POOL_WINDOWS) // 2


def _pool_kernel(z_ref, o_ref, pad_ref, *maybe_shifted, dil):
    t = z_ref.shape[0]
    npos = t // dil
    padr = POOL_HALF_MAX * dil
    zeros = jnp.zeros((padr, WB), F32)
    pad_ref[pl.ds(0, padr), :] = zeros
    pad_ref[pl.ds(padr, t), :] = z_ref[...]
    pad_ref[pl.ds(padr + t, padr), :] = zeros
    if maybe_shifted:
        (shifted_ref,) = maybe_shifted
        nshift = pad_ref.shape[0] - SUBLANES
        for rho in range(1, SUBLANES):
            shifted_ref[rho - 1, pl.ds(0, nshift), :] = pad_ref[pl.ds(rho, nshift), :]

    def rows_at(start, lanes):
        rho = start % SUBLANES
        if rho == 0 or not maybe_shifted:
            return pad_ref[pl.ds(start, POOL_ROWS), lanes]
        return shifted_ref[rho - 1, pl.ds(start - rho, POOL_ROWS), lanes]

    for c in range(t // POOL_ROWS):
        r0 = c * POOL_ROWS
        pos = (r0 + lax.broadcasted_iota(jnp.int32, (POOL_ROWS, POOL_GC), 0)) >> (dil.bit_length() - 1)
        for g, win in enumerate(POOL_WINDOWS):
            lanes = pl.ds(g * POOL_GC, POOL_GC)
            acc = None
            for o in range(-(win // 2), win - win // 2):
                term = rows_at(padr + r0 + o * dil, lanes)
                acc = term if acc is None else acc + term
            lo = jnp.maximum(pos - win // 2, 0)
            hi = jnp.minimum(pos - win // 2 + win, npos)
            cnt = (hi - lo).astype(F32)
            x = pad_ref[pl.ds(padr + r0, POOL_ROWS), lanes]
            o_ref[pl.ds(r0, POOL_ROWS), lanes] = (acc / cnt - x).astype(o_ref.dtype)


def _pool(z, row0, rows, t, dil, prev):
    blk0 = row0 // t
    pre_args, pre_specs, aliases = _into(prev)
    return pl.pallas_call(
        _skip_refs(functools.partial(_pool_kernel, dil=dil), len(pre_args)),
        grid=(rows // t,),
        in_specs=pre_specs + [pl.BlockSpec((t, WB), lambda i: (blk0 + i, Z_POOL // WB))],
        out_specs=pl.BlockSpec((t, WB), lambda i: (blk0 + i, 0)),
        out_shape=jax.ShapeDtypeStruct((z.shape[0], WB), BF16),
        input_output_aliases=aliases,
        scratch_shapes=[pltpu.VMEM((t + 2 * POOL_HALF_MAX * dil, WB), F32)]
        + ([] if dil % SUBLANES == 0 else [pltpu.VMEM((SUBLANES - 1, t + 2 * POOL_HALF_MAX * dil, WB), F32)]),
        compiler_params=_params("parallel"),
        name=f"pool_dil{dil}",
    )(*pre_args, z)


SCAN_CHUNK = 256
NSTREAM = 2 * N_HEADS
N_COLVEC = 4


def _log_sigmoid(x):
    return jnp.minimum(x, 0.0) - jnp.log1p(jnp.exp(-jnp.abs(x)))


def _lane_scan(x, op, ident, reverse):
    lane = lax.broadcasted_iota(jnp.int32, x.shape, 1)
    n = x.shape[1]
    sh = 1
    while sh < n:
        if reverse:
            x = op(x, jnp.where(lane < n - sh, pltpu.roll(x, n - sh, 1), ident))
        else:
            x = op(x, jnp.where(lane >= sh, pltpu.roll(x, sh, 1), ident))
        sh *= 2
    return x


def _mlstm_kernel(*refs, zero_init, emit_state):
    q_ref, k_ref, v_ref, og_ref, g_ref, nw_ref = refs[:6]
    pos = 6
    if not zero_init:
        c0_ref, n0_ref, m0_ref = refs[pos:pos + 3]
        pos += 3
    o_ref = refs[pos]
    pos += 1
    if emit_state:
        co_ref, no_ref, mo_ref = refs[pos:pos + 3]
        pos += 3
    rowbuf, decbuf, colbuf, h_scr = refs[pos:pos + 4]
    c_scr = refs[pos + 4:pos + 4 + NSTREAM]
    n_scr = refs[pos + 4 + NSTREAM:pos + 4 + 2 * NSTREAM]

    t = SCAN_CHUNK
    nsteps = q_ref.shape[0] // t

    log_i = g_ref[0]
    log_f = _log_sigmoid(g_ref[1])
    backward = (lax.broadcasted_iota(jnp.int32, log_i.shape, 0) & N_HEADS) != 0

    def scan(x, op, ident):
        return jnp.where(backward, _lane_scan(x, op, ident, True), _lane_scan(x, op, ident, False))

    b = scan(log_f, jnp.add, 0.0)
    c = log_i - b
    c_run = scan(c, jnp.maximum, -jnp.inf)
    b_tot = jnp.sum(log_f, axis=1, keepdims=True)
    c_tot = jnp.max(c, axis=1, keepdims=True)
    m = jnp.zeros((NSTREAM, 1), F32) if zero_init else m0_ref[...]
    unused_rows = jnp.zeros((LANES - N_COLVEC * NSTREAM, t), F32)
    for i in range(nsteps):
        sl = slice(i * NSTREAM, (i + 1) * NSTREAM)
        m_last = jnp.maximum(m, c_tot[sl])
        m_run = jnp.maximum(m, c_run[sl])
        packed = [m_run, jnp.exp(m - m_run), jnp.exp(-(b[sl] + m_run)), jnp.exp(c[sl] - m_last), unused_rows]
        colbuf[i] = jnp.concatenate(packed, axis=0).T
        rowbuf[i] = c[sl]
        decbuf[i] = jnp.broadcast_to(jnp.exp(m - m_last), (NSTREAM, HEAD_DIM))
        m = b_tot[sl] + m_last
    if emit_state:
        mo_ref[...] = m

    if not zero_init:
        for s in range(NSTREAM):
            d, hd = divmod(s, N_HEADS)
            c_scr[s][...] = c0_ref[d, hd]
            n_scr[s][...] = n0_ref[s:s + 1, :]

    t_idx = lax.broadcasted_iota(jnp.int32, (t, t), 0)
    s_idx = lax.broadcasted_iota(jnp.int32, (t, t), 1)
    scanned = (s_idx <= t_idx, s_idx >= t_idx)

    def step(i, has_state, update_state):
        col = colbuf[i]

        def colvec(vec, s):
            return col[:, vec * NSTREAM + s:vec * NSTREAM + s + 1]

        def where(s):
            d, hd = divmod(s, N_HEADS)
            cidx = i if d == 0 else nsteps - 1 - i
            return d, pl.ds(pl.multiple_of(cidx * t, t), t), pl.ds(hd * HEAD_DIM, HEAD_DIM)

        scores = []
        for s in range(NSTREAM):
            d, rows, cols = where(s)
            dmat = jnp.exp(jnp.where(scanned[d], rowbuf[i, s:s + 1, :] - colvec(0, s), -jnp.inf))
            q = q_ref[rows, cols] * (HEAD_DIM ** -0.5)
            scores.append(lax.dot_general(q, k_ref[rows, cols], (((1,), (1,)), ((), ())),
                                          preferred_element_type=F32) * dmat)
        for s in range(NSTREAM):
            d, rows, cols = where(s)
            num = jnp.dot(scores[s], v_ref[rows, cols], preferred_element_type=F32)
            den = jnp.sum(scores[s], axis=1, keepdims=True)
            if has_state:
                q = q_ref[rows, cols] * (HEAD_DIM ** -0.5)
                w_int = colvec(1, s)
                num = num + w_int * jnp.dot(q, c_scr[s][...], preferred_element_type=F32)
                den = den + w_int * jnp.sum(q * n_scr[s][...], axis=1, keepdims=True)
            h_scr[d, rows, cols] = num * (1.0 / jnp.maximum(jnp.abs(den), colvec(2, s)))
        if update_state:
            for s in range(NSTREAM):
                d, rows, cols = where(s)
                wk = k_ref[rows, cols] * colvec(3, s)
                c_new = lax.dot_general(wk, v_ref[rows, cols], (((0,), (0,)), ((), ())),
                                        preferred_element_type=F32)
                n_new = jnp.sum(wk, axis=0, keepdims=True)
                if has_state:
                    dec = decbuf[i, s:s + 1, :]
                    c_new = c_new + dec * c_scr[s][...]
                    n_new = n_new + dec * n_scr[s][...]
                c_scr[s][...] = c_new
                n_scr[s][...] = n_new

    first = 1 if zero_init else 0
    last = nsteps if emit_state else nsteps - 1
    if zero_init:
        step(0, False, emit_state or nsteps > 1)
    if last - first == 1:
        step(first, True, True)
    elif last > first:
        def body(i, carry):
            step(i, True, True)
            return carry
        lax.fori_loop(first, last, body, 0)
    if not emit_state and not (zero_init and nsteps == 1):
        step(nsteps - 1, True, False)

    for hd in range(N_HEADS):
        cols = pl.ds(hd * HEAD_DIM, HEAD_DIM)
        hn = _rms(h_scr[0, :, cols] + h_scr[1, :, cols])
        o_ref[:, cols] = (hn * nw_ref[:, cols] * jax.nn.sigmoid(og_ref[:, cols])).astype(o_ref.dtype)
    if emit_state:
        for s in range(NSTREAM):
            d, hd = divmod(s, N_HEADS)
            co_ref[d, hd] = c_scr[s][...]
            no_ref[s:s + 1, :] = n_scr[s][...]


def _mlstm(z, gates, norm_w, row0, nb, seq, init_state, layer, prev_hc, state_out):
    m_tot = z.shape[0]
    t = SCAN_CHUNK
    nsteps = seq // t
    blk0 = row0 // seq
    nrow = nsteps * NSTREAM
    g = gates.reshape(nb, nsteps, t, 2, 2, N_HEADS).transpose(0, 4, 3, 1, 5, 2)
    g = jnp.concatenate([g[:, :, 0], g[:, :, 1, ::-1]], axis=3)
    g = g.reshape(nb, 2, nrow, t)

    big = seq * WC * 4 > (2 << 20)

    def zspec(off):
        idx = lambda b: (blk0 + b, off // WC)
        return pl.BlockSpec((seq, WC), idx, pipeline_mode=pl.Buffered(1)) if big else pl.BlockSpec((seq, WC), idx)

    in_specs = [zspec(Z_QKV), zspec(Z_QKV + WC), zspec(Z_QKV + 2 * WC), zspec(Z_OG),
                pl.BlockSpec((None, 2, nrow, t), lambda b: (b, 0, 0, 0)),
                pl.BlockSpec((1, WC), lambda b: (0, 0))]
    args = [z, z, z, z, g, norm_w.reshape(1, WC)]
    if init_state is not None:
        sc, sn, sm = init_state
        depth = sc.shape[1]
        in_specs += [pl.BlockSpec((None, None, 2, N_HEADS, HEAD_DIM, HEAD_DIM), lambda b: (b, layer, 0, 0, 0, 0)),
                     pl.BlockSpec((None, None, NSTREAM, HEAD_DIM), lambda b: (b, layer, 0, 0)),
                     pl.BlockSpec((None, None, NSTREAM, 1), lambda b: (b, layer, 0, 0))]
        args += [sc, sn.reshape(nb, depth, NSTREAM, HEAD_DIM), sm.reshape(nb, depth, NSTREAM, 1)]
    out_specs = [pl.BlockSpec((seq, WC), lambda b: (blk0 + b, 0))]
    out_shape = [jax.ShapeDtypeStruct((m_tot, WC), BF16)]
    aliases = {}
    pre_args, pre_specs = [], []
    if prev_hc is not None:
        pre_args.append(prev_hc)
        pre_specs.append(pl.BlockSpec(memory_space=pl.ANY))
        aliases[0] = 0
    if state_out is not None:
        depth, prev_c = state_out
        out_specs += [pl.BlockSpec((None, None, 2, N_HEADS, HEAD_DIM, HEAD_DIM), lambda b: (b, layer, 0, 0, 0, 0)),
                      pl.BlockSpec((None, NSTREAM, HEAD_DIM), lambda b: (b, 0, 0)),
                      pl.BlockSpec((None, NSTREAM, 1), lambda b: (b, 0, 0))]
        out_shape += [jax.ShapeDtypeStruct((nb, depth, 2, N_HEADS, HEAD_DIM, HEAD_DIM), F32),
                      jax.ShapeDtypeStruct((nb, NSTREAM, HEAD_DIM), F32),
                      jax.ShapeDtypeStruct((nb, NSTREAM, 1), F32)]
        if prev_c is not None:
            aliases[len(pre_args)] = 1
            pre_args.append(prev_c)
            pre_specs.append(pl.BlockSpec(memory_space=pl.ANY))
    body = functools.partial(_mlstm_kernel, zero_init=init_state is None, emit_state=state_out is not None)
    return pl.pallas_call(
        _skip_refs(body, len(pre_args)),
        grid=(nb,),
        in_specs=pre_specs + in_specs,
        out_specs=out_specs,
        out_shape=out_shape,
        input_output_aliases=aliases,
        scratch_shapes=[pltpu.VMEM((nsteps, NSTREAM, t), F32),
                        pltpu.VMEM((nsteps, NSTREAM, HEAD_DIM), F32),
                        pltpu.VMEM((nsteps, t, LANES), F32),
                        pltpu.VMEM((2, seq, WC), F32)]
        + [pltpu.VMEM((HEAD_DIM, HEAD_DIM), F32)] * NSTREAM + [pltpu.VMEM((1, HEAD_DIM), F32)] * NSTREAM,
        compiler_params=_params("parallel"),
        name=f"mlstm_seq{seq}",
    )(*pre_args, *args)


def _merge_kernel(h_ref, a_ref, p_ref, c_ref, wga_ref, wgb_ref, wgc_ref, bga_ref, bgb_ref, bgc_ref,
                  wa_ref, wp_ref, ps_ref, wc_ref, o_ref):
    h = h_ref[...]

    def gate(w_ref, b_ref):
        return jax.nn.sigmoid(lax.dot_general(h, w_ref[...], _NT, preferred_element_type=F32) + b_ref[...])

    ya = jnp.dot(a_ref[...], wa_ref[...], preferred_element_type=F32)
    yb = jnp.dot(p_ref[...], wp_ref[...], preferred_element_type=F32) * ps_ref[...]
    yc = jnp.dot(c_ref[...], wc_ref[...], preferred_element_type=F32)
    mix = gate(wga_ref, bga_ref) * ya + gate(wgb_ref, bgb_ref) * yb + gate(wgc_ref, bgc_ref) * yc
    o_ref[...] = mix.astype(o_ref.dtype)


def _merge(h, a_act, p, hc, w_gates_t, b_gates, layer, w_a_out, w_pool, pool_scale, w_c_out):
    m, d = h.shape
    tm = 1024
    tn = d // N_POOL_GROUPS
    nj = d // tn

    def wg(k):
        return pl.BlockSpec((None, tn, d), lambda i, j: (layer, k * nj + j, 0))

    def bg(k):
        return pl.BlockSpec((None, 1, tn), lambda i, j: (layer, 0, k * nj + j))

    return pl.pallas_call(
        _merge_kernel,
        grid=(m // tm, nj),
        in_specs=[pl.BlockSpec((tm, d), lambda i, j: (i, 0)),
                  pl.BlockSpec((tm, WA), lambda i, j: (i, 0)),
                  pl.BlockSpec((tm, POOL_GC), lambda i, j: (i, j)),
                  pl.BlockSpec((tm, WC), lambda i, j: (i, 0)),
                  wg(0), wg(1), wg(2), bg(0), bg(1), bg(2),
                  pl.BlockSpec((None, WA, tn), lambda i, j: (layer, 0, j)),
                  pl.BlockSpec((None, None, POOL_GC, tn), lambda i, j: (layer, j, 0, 0)),
                  pl.BlockSpec((1, tn), lambda i, j: (0, j)),
                  pl.BlockSpec((None, WC, tn), lambda i, j: (layer, 0, j))],
        out_specs=pl.BlockSpec((tm, tn), lambda i, j: (i, j)),
        out_shape=jax.ShapeDtypeStruct((m, d), BF16),
        compiler_params=_params("parallel", "arbitrary"),
        name="merge",
    )(h, a_act, p, hc, w_gates_t, w_gates_t, w_gates_t, b_gates, b_gates, b_gates,
      w_a_out, w_pool, pool_scale.reshape(1, d), w_c_out)


def _proj_residual_kernel(*refs, sub, emit_next):
    a_ref, w_ref, x_ref, wpost_ref, gate_ref = refs[:5]
    if emit_next:
        wpre_ref, sc_ref, sh_ref, xo_ref, ho_ref = refs[5:]
    else:
        (xo_ref,) = refs[5:]
    for r in range(a_ref.shape[0] // sub):
        rows = pl.ds(r * sub, sub)
        y = jnp.dot(a_ref[rows, :], w_ref[...], preferred_element_type=F32)
        xn = x_ref[rows, :] + gate_ref[...] * (_rms(y) * wpost_ref[...])
        xo_ref[rows, :] = xn
        if emit_next:
            hn = _rms(xn) * wpre_ref[...]
            ho_ref[rows, :] = (hn * (1.0 + sc_ref[...]) + sh_ref[...]).astype(ho_ref.dtype)


def _proj_residual(lay, a, w, layer, x, w_post, mod, gate, tm, sub, name, nxt=None, row0=0, rows=None):
    d = lay.d
    kdim = a.shape[1]
    rows = lay.m if rows is None else rows
    blk0 = row0 // tm
    row = pl.BlockSpec((1, d), lambda i: (0, 0))
    tile = lambda width: pl.BlockSpec((tm, width), lambda i: (blk0 + i, 0))

    def mod_at(layer_k, k):
        spec = lay.mod_spec(layer_k, k, tm)
        return pl.BlockSpec(spec.block_shape, lambda i: spec.index_map(blk0 + i))

    in_specs = [tile(kdim),
                pl.BlockSpec((None, kdim, d), lambda i: (layer, 0, 0), pipeline_mode=pl.Buffered(1)),
                tile(d), row, mod_at(*gate)]
    args = [a, w, x, w_post.reshape(1, d), mod]
    out_tile = pl.BlockSpec((tm, d), lambda i: (i, 0))
    out_specs, out_shape = [out_tile], [jax.ShapeDtypeStruct((rows, d), F32)]
    if nxt is not None:
        assert row0 == 0 and rows == lay.m
        w_pre_next, scale, shift = nxt
        in_specs += [row, mod_at(*scale), mod_at(*shift)]
        args += [w_pre_next.reshape(1, d), mod, mod]
        out_specs.append(out_tile)
        out_shape.append(jax.ShapeDtypeStruct((rows, d), BF16))
    body = functools.partial(_proj_residual_kernel, sub=sub, emit_next=nxt is not None)
    return pl.pallas_call(
        body,
        grid=(rows // tm,),
        in_specs=in_specs,
        out_specs=out_specs,
        out_shape=out_shape,
        compiler_params=_params("parallel"),
        name=name,
    )(*args)


def _gelu_tanh(x):
    return 0.5 * x * (1.0 + jnp.tanh(0.7978845608028654 * (x + 0.044715 * (x * x * x))))


MXU_WIDTH = 256


def _ffn_up_kernel(h_ref, wu_ref, wg_ref, cw_ref, cb_ref, o_ref, *, shift, seg):
    h = h_ref[...]
    tm = h.shape[0]
    blocks = [pl.ds(c * MXU_WIDTH, MXU_WIDTH) for c in range(o_ref.shape[1] // MXU_WIDTH)]

    def activation(cols, g):
        if shift % SUBLANES == 0 and seg == tm:
            zeros = jnp.zeros((shift, MXU_WIDTH), F32)
            prev = jnp.concatenate([zeros, g[:tm - shift]], axis=0)
            nxt = jnp.concatenate([g[shift:], zeros], axis=0)
        else:
            r = lax.broadcasted_iota(jnp.int32, g.shape, 0) & (seg - 1)
            prev = jnp.where(r < shift, 0.0, pltpu.roll(g, shift, 0))
            nxt = jnp.where(r >= seg - shift, 0.0, pltpu.roll(g, tm - shift, 0))
        gc = cw_ref[0:1, cols] * prev + cw_ref[1:2, cols] * g + cw_ref[2:3, cols] * nxt + cb_ref[:, cols]
        return _gelu_tanh(gc)

    gates = [jnp.dot(h, wg_ref[:, cols], preferred_element_type=F32) for cols in blocks]
    acts = [activation(cols, g) for cols, g in zip(blocks, gates)]
    for cols, act in zip(blocks, acts):
        u = jnp.dot(h, wu_ref[:, cols], preferred_element_type=F32)
        o_ref[:, cols] = (act * u).astype(o_ref.dtype)


def _ffn_up(h2, row0, rows, tm, shift, seg, w_up, layer, conv_w, conv_b, prev):
    m, d = h2.shape
    dff = w_up.shape[2] // 2
    tn = 512
    nj = dff // tn
    blk0 = row0 // tm
    pre_args, pre_specs, aliases = _into(prev)
    return pl.pallas_call(
        _skip_refs(functools.partial(_ffn_up_kernel, shift=shift, seg=seg), len(pre_args)),
        grid=(rows // tm, nj),
        in_specs=pre_specs + [pl.BlockSpec((tm, d), lambda i, j: (blk0 + i, 0)),
                              pl.BlockSpec((None, d, tn), lambda i, j: (layer, 0, j)),
                              pl.BlockSpec((None, d, tn), lambda i, j: (layer, 0, nj + j)),
                              pl.BlockSpec((3, tn), lambda i, j: (0, j)),
                              pl.BlockSpec((1, tn), lambda i, j: (0, j))],
        out_specs=pl.BlockSpec((tm, tn), lambda i, j: (blk0 + i, j)),
        out_shape=jax.ShapeDtypeStruct((m, dff), BF16),
        input_output_aliases=aliases,
        compiler_params=_params("parallel", "arbitrary"),
        name=f"ffn_up_shift{shift}",
    )(*pre_args, h2, w_up, w_up, conv_w, conv_b.reshape(1, dff))


def kernel(x_prompt, x_sample, state_C, state_n, state_m, c, c_ctx, w_ada, b_ada, norm_mix_pre, norm_mix_post, norm_ffn_pre, norm_ffn_post, w_in, b_in, conv_a_w, conv_a_b, ln_a_w, ln_a_b, w_a_out, w_pool, pool_scale, mlstm_norm_w, w_c_out, w_out, w_ffn_up, ffn_conv_w, ffn_conv_b, w_ffn_down):
    bp, lp, d = x_prompt.shape
    bs, ls, _ = x_sample.shape
    depth = w_ada.shape[0]
    lay = _Layout(bp, lp, bs, ls, d)
    assert bs < N_COND_ROWS and lay.mp % (2 * ls) == 0 and lay.ms % (2 * ls) == 0 and ls % lp == 0
    assert lp & (lp - 1) == 0 and GRID_W & (GRID_W - 1) == 0
    assert lp % SCAN_CHUNK == 0 and ls % SCAN_CHUNK == 0 and N_COLVEC * NSTREAM <= LANES

    cond = jnp.zeros((N_COND_ROWS, d), F32).at[:bs].set(c).at[bs].set(c_ctx)
    mod = _ada(cond, w_ada, b_ada).reshape(depth * N_COND_ROWS * N_MOD, 1, d)

    tc = 512
    ngate = OFF_MERGE - OFF_GATES
    assert OFF_QKV % tc == 0 and OFF_GATES % tc == 0 and ngate % 16 == 0 and ngate < tc
    nz = OFF_GATES // tc
    w_in_t = jnp.swapaxes(w_in, 1, 2)
    w_z = _cast_rows(w_in_t, lambda j: (j + OFF_QKV // tc) % nz, nz, 0, tc, "cast_w_z")
    w_m =_cast_rows(w_in_t, lambda j: nz + j, N_BRANCH * d // tc, ngate, tc, "cast_w_merge")
    gate_pad = (0, LANES - ngate)
    w_g = jnp.pad(w_in_t[:, OFF_GATES:OFF_MERGE, :], ((0, 0), gate_pad, (0, 0))).astype(BF16)
    b_z = jnp.concatenate([b_in[:, OFF_QKV:OFF_GATES], b_in[:, :OFF_QKV]], axis=-1)[:, None, :]
    b_m = b_in[:, None, OFF_MERGE:]
    b_g = jnp.pad(b_in[:, OFF_GATES:OFF_MERGE], ((0, 0), gate_pad))[:, None, :]
    w_a_out_b = w_a_out.astype(BF16)
    w_pool_b = w_pool.astype(BF16)
    w_c_out_b = w_c_out.astype(BF16)
    w_out_b = w_out.astype(BF16)
    w_up_b = w_ffn_up.astype(BF16)
    w_down_b = w_ffn_down.astype(BF16)

    x, h, zg = _entry(lay, x_prompt.reshape(lay.mp, d), x_sample.reshape(lay.ms, d), norm_mix_pre[0], mod, w_g, b_g)
    new_c = None
    new_n, new_m = [], []
    for l in range(depth):
        z = _matmul_bias(h, w_z, b_z, l, Z_WIDTH, 2048, 512, "in_proj")
        zg = zg[:, :4 * N_HEADS]

        conv_args = (conv_a_w[l], conv_a_b[l], ln_a_w[l], ln_a_b[l])
        a_act = _conv_a(z, 0, lay.mp, lp, *conv_args, None)
        a_act = _conv_a(z, lay.mp, lay.ms, GRID_W, *conv_args, a_act)
        p = _pool(z, 0, lay.mp, lp, 1, None)
        p = _pool(z, lay.mp, lay.ms, ls, GRID_W, p)
        hc, new_c, n_l, m_l = _mlstm(z, zg[:lay.mp], mlstm_norm_w[l], 0, bp, lp, None, l, None, (depth, new_c))
        (hc,) = _mlstm(z, zg[lay.mp:], mlstm_norm_w[l], lay.mp, bs, ls, (state_C, state_n, state_m), l, hc, None)
        new_n.append(n_l.reshape(bp, 2, N_HEADS, HEAD_DIM))
        new_m.append(m_l.reshape(bp, 2, N_HEADS))

        mix_in = _merge(h, a_act, p, hc, w_m, b_m, l, w_a_out_b, w_pool_b, pool_scale[l], w_c_out_b)
        x, h2 = _proj_residual(lay, mix_in, w_out_b, l, x, norm_mix_post[l], mod, (l, 2), 512, 256, "out_proj",
                               nxt=(norm_ffn_pre[l], (l, 4), (l, 3)))
        ffn_args = (w_up_b, l, ffn_conv_w[l], ffn_conv_b[l])
        tm_ffn = ls
        f_in = _ffn_up(h2, 0, lay.mp, tm_ffn, 1, lp, *ffn_args, None)
        f_in = _ffn_up(h2, lay.mp, lay.ms, tm_ffn, GRID_W, ls, *ffn_args, f_in)
        down_args = (lay, f_in, w_down_b, l, x, norm_ffn_post[l], mod, (l, 5), 256, 128, "ffn_down")
        if l + 1 < depth:
            x, h = _proj_residual(*down_args, nxt=(norm_mix_pre[l + 1], (l + 1, 1), (l + 1, 0)))
            zg = _matmul_bias(h, w_g, b_g, l + 1, LANES, 1024, LANES, "gate_proj")
        else:
            (y_prompt,) = _proj_residual(*down_args, row0=0, rows=lay.mp)
            (y_sample,) = _proj_residual(*down_args, row0=lay.mp, rows=lay.ms)
            y_prompt, y_sample = y_prompt.reshape(bp, lp, d), y_sample.reshape(bs, ls, d)
    return (y_prompt, y_sample, new_c, jnp.stack(new_n, axis=1), jnp.stack(new_m, axis=1))
```
